```python
import math
import jax, jax.numpy as jnp
from jax import lax
import numpy as np

D_MODEL = 1024
BATCH = 8
SEQ = 2048
DEPTH = 1

N_MEM = 256
CONV_DIM = 512
CONV_WIDTH = 31
N_HEADS = 8
N_KV_HEADS = 2
HEAD_DIM = 64
WINDOW = 128
BLOCK = 128
N_MEM_HEADS = 4
MEM_HEAD_DIM = 128
N_BUCKETS = 32
MAX_DISTANCE = 128
N_BRANCHES = 3
D_FF = int(math.ceil(8 * D_MODEL / 3 / 256)) * 256
DEEPNORM_ALPHA = (2 * DEPTH) ** 0.25
DEEPNORM_BETA = (8 * DEPTH) ** -0.25
LN_EPS = 1e-5
NEG_INF = -1e30

ATTN_Q_DIM = N_HEADS * HEAD_DIM
KV_DIM = N_KV_HEADS * HEAD_DIM
MEM_DIM = N_MEM_HEADS * MEM_HEAD_DIM
IN_WIDTHS = (2 * CONV_DIM, ATTN_Q_DIM, KV_DIM, KV_DIM, MEM_DIM, N_BRANCHES * D_MODEL)
IN_DIM = sum(IN_WIDTHS)

kernel_name = "hybrid_conv_window_gqa_memory_encoder"


def split_points():
    pts, acc = [], 0
    for w in IN_WIDTHS[:-1]:
        acc += w
        pts.append(acc)
    return pts


def layer_norm(x, g, b):
    xf = x.astype(jnp.float32)
    mu = jnp.mean(xf, axis=-1, keepdims=True)
    var = jnp.mean(jnp.square(xf - mu), axis=-1, keepdims=True)
    return ((xf - mu) * lax.rsqrt(var + LN_EPS)).astype(x.dtype) * g + b


def t5_bucket(rel):
    half = N_BUCKETS // 2
    max_exact = half // 2
    base = jnp.where(rel > 0, half, 0)
    n = jnp.abs(rel)
    nf = jnp.maximum(n, 1).astype(jnp.float32)
    large = max_exact + (jnp.log(nf / max_exact) / math.log(MAX_DISTANCE / max_exact)
                         * (half - max_exact)).astype(jnp.int32)
    large = jnp.minimum(large, half - 1)
    return base + jnp.where(n < max_exact, n, large)


def conformer_conv(glu_in, dw_w, dw_b, ln_g, ln_b, w_out):
    a, g = jnp.split(glu_in, 2, axis=-1)
    u = a * jax.nn.sigmoid(g)
    pad = CONV_WIDTH // 2
    u = lax.conv_general_dilated(u, dw_w[:, None, :], window_strides=(1,),
                                 padding=[(pad, pad)],
                                 dimension_numbers=('NWC', 'WIO', 'NWC'),
                                 feature_group_count=CONV_DIM) + dw_b
    u = jax.nn.silu(layer_norm(u, ln_g, ln_b))
    return u @ w_out


def windowed_gqa(q, k, v, rel_bias, sink):
    B, S = q.shape[0], q.shape[1]
    nb = S // BLOCK
    G = N_HEADS // N_KV_HEADS
    qb = q.reshape(B, nb, BLOCK, N_KV_HEADS, G, HEAD_DIM)
    pad = ((0, 0), (BLOCK, BLOCK), (0, 0), (0, 0))
    kr = jnp.pad(k, pad).reshape(B, nb + 2, BLOCK, N_KV_HEADS, HEAD_DIM)
    vr = jnp.pad(v, pad).reshape(B, nb + 2, BLOCK, N_KV_HEADS, HEAD_DIM)
    kb = jnp.concatenate([kr[:, :-2], kr[:, 1:-1], kr[:, 2:]], axis=2)
    vb = jnp.concatenate([vr[:, :-2], vr[:, 1:-1], vr[:, 2:]], axis=2)

    qloc = jnp.arange(BLOCK, dtype=jnp.int32)
    kloc = jnp.arange(3 * BLOCK, dtype=jnp.int32) - BLOCK
    rel = kloc[None, :] - qloc[:, None]
    bias = rel_bias[t5_bucket(rel)].astype(jnp.float32)
    bias = jnp.transpose(bias, (2, 0, 1)).reshape(N_KV_HEADS, G, BLOCK, 3 * BLOCK)
    kpos = jnp.arange(nb, dtype=jnp.int32)[:, None, None] * BLOCK + kloc[None, None, :]
    valid = (jnp.abs(rel) <= WINDOW)[None] & (kpos >= 0) & (kpos < S)

    scale = HEAD_DIM ** -0.5
    s = jnp.einsum('bnqhgd,bnkhd->bnhgqk', qb, kb,
                   preferred_element_type=jnp.float32) * scale + bias
    s = jnp.where(valid[None, :, None, None], s, NEG_INF)
    sink_col = jnp.broadcast_to(sink.astype(jnp.float32).reshape(N_KV_HEADS, G, 1, 1),
                                s.shape[:-1] + (1,))
    p = jax.nn.softmax(jnp.concatenate([s, sink_col], axis=-1), axis=-1)[..., :-1]
    o = jnp.einsum('bnhgqk,bnkhd->bnqhgd', p.astype(v.dtype), vb)
    return o.reshape(B, S, N_HEADS * HEAD_DIM)


def memory_attention(q_mem, mem, w_mem_kv):
    B, S = q_mem.shape[0], q_mem.shape[1]
    q = q_mem.reshape(B, S, N_MEM_HEADS, MEM_HEAD_DIM)
    km, vm = jnp.split(mem @ w_mem_kv, 2, axis=-1)
    km = km.reshape(B, N_MEM, N_MEM_HEADS, MEM_HEAD_DIM)
    vm = vm.reshape(B, N_MEM, N_MEM_HEADS, MEM_HEAD_DIM)
    s = jnp.einsum('bshd,bmhd->bhsm', q, km,
                   preferred_element_type=jnp.float32) * (MEM_HEAD_DIM ** -0.5)
    p = jax.nn.softmax(s, axis=-1)
    o = jnp.einsum('bhsm,bmhd->bshd', p.astype(vm.dtype), vm)
    return o.reshape(B, S, MEM_DIM)


def hybrid_layer(x, mem, rel_bias, w_in, b_gate, conv_dw_w, conv_dw_b, conv_ln_g, conv_ln_b,
                 w_conv_out, attn_sink, w_attn_out, w_mem_kv, w_mem_out, w_o,
                 ln1_g, ln1_b, w_ffn_in, w_ffn_out, ln2_g, ln2_b):
    B, S, D = x.shape
    proj = x @ w_in
    glu_in, q, k, v, q_mem, gate_logits = jnp.split(proj, split_points(), axis=-1)

    y_conv = conformer_conv(glu_in, conv_dw_w, conv_dw_b, conv_ln_g, conv_ln_b, w_conv_out)
    y_attn = windowed_gqa(q.reshape(B, S, N_HEADS, HEAD_DIM),
                          k.reshape(B, S, N_KV_HEADS, HEAD_DIM),
                          v.reshape(B, S, N_KV_HEADS, HEAD_DIM),
                          rel_bias, attn_sink) @ w_attn_out
    y_mem = memory_attention(q_mem, mem, w_mem_kv) @ w_mem_out

    gates = jax.nn.sigmoid(gate_logits + b_gate).reshape(B, S, N_BRANCHES, D)
    merged = gates[:, :, 0] * y_conv + gates[:, :, 1] * y_attn + gates[:, :, 2] * y_mem
    x = layer_norm(DEEPNORM_ALPHA * x + merged @ w_o, ln1_g, ln1_b)

    gate, up = jnp.split(x @ w_ffn_in, 2, axis=-1)
    ffn = (jax.nn.silu(gate) * up) @ w_ffn_out
    return layer_norm(DEEPNORM_ALPHA * x + ffn, ln2_g, ln2_b)


def setup_inputs(seed: int = 0) -> dict:
    key = jax.random.key(seed)
    ks = jax.random.split(key, 24)
    f32 = jnp.float32
    L = DEPTH

    def nrm(k, shape, scale):
        return jax.random.normal(k, shape, f32) * scale

    beta = DEEPNORM_BETA
    return {
        "x": nrm(ks[0], (BATCH, SEQ, D_MODEL), 1.0),
        "mem": nrm(ks[1], (BATCH, N_MEM, D_MODEL), 1.0),
        "rel_bias": nrm(ks[2], (N_BUCKETS, N_HEADS), 0.5),
        "w_in": nrm(ks[3], (L, D_MODEL, IN_DIM), D_MODEL ** -0.5),
        "b_gate": nrm(ks[4], (L, N_BRANCHES * D_MODEL), 0.1),
        "conv_dw_w": nrm(ks[5], (L, CONV_WIDTH, CONV_DIM), CONV_WIDTH ** -0.5),
        "conv_dw_b": nrm(ks[6], (L, CONV_DIM), 0.02),
        "conv_ln_g": 1.0 + nrm(ks[7], (L, CONV_DIM), 0.02),
        "conv_ln_b": nrm(ks[8], (L, CONV_DIM), 0.02),
        "w_conv_out": nrm(ks[9], (L, CONV_DIM, D_MODEL), beta * CONV_DIM ** -0.5),
        "attn_sink": nrm(ks[10], (L, N_HEADS), 0.5),
        "w_attn_out": nrm(ks[11], (L, ATTN_Q_DIM, D_MODEL), beta * ATTN_Q_DIM ** -0.5),
        "w_mem_kv": nrm(ks[12], (L, D_MODEL, 2 * MEM_DIM), D_MODEL ** -0.5),
        "w_mem_out": nrm(ks[13], (L, MEM_DIM, D_MODEL), beta * MEM_DIM ** -0.5),
        "w_o": nrm(ks[14], (L, D_MODEL, D_MODEL), beta * D_MODEL ** -0.5),
        "ln1_g": 1.0 + nrm(ks[15], (L, D_MODEL), 0.02),
        "ln1_b": nrm(ks[16], (L, D_MODEL), 0.02),
        "w_ffn_in": nrm(ks[17], (L, D_MODEL, 2 * D_FF), D_MODEL ** -0.5),
        "w_ffn_out": nrm(ks[18], (L, D_FF, D_MODEL), beta * D_FF ** -0.5),
        "ln2_g": 1.0 + nrm(ks[19], (L, D_MODEL), 0.02),
        "ln2_b": nrm(ks[20], (L, D_MODEL), 0.02),
    }


def reference(x, mem, rel_bias, w_in, b_gate, conv_dw_w, conv_dw_b, conv_ln_g, conv_ln_b,
              w_conv_out, attn_sink, w_attn_out, w_mem_kv, w_mem_out, w_o,
              ln1_g, ln1_b, w_ffn_in, w_ffn_out, ln2_g, ln2_b):
    for l in range(DEPTH):
        x = hybrid_layer(x, mem, rel_bias, w_in[l], b_gate[l], conv_dw_w[l], conv_dw_b[l],
                         conv_ln_g[l], conv_ln_b[l], w_conv_out[l], attn_sink[l],
                         w_attn_out[l], w_mem_kv[l], w_mem_out[l], w_o[l],
                         ln1_g[l], ln1_b[l], w_ffn_in[l], w_ffn_out[l], ln2_g[l], ln2_b[l])
    return x
```

```python
import functools
import math

import jax
import jax.numpy as jnp
from jax import lax
from jax.experimental import pallas as pl
from jax.experimental.pallas import tpu as pltpu

D_MODEL = 1024
N_MEM = 256
CONV_DIM = 512
CONV_WIDTH = 31
N_HEADS = 8
N_KV_HEADS = 2
HEAD_DIM = 64
WINDOW = 128
BLOCK = 128
N_MEM_HEADS = 4
MEM_HEAD_DIM = 128
N_BUCKETS = 32
MAX_DISTANCE = 128
N_BRANCHES = 3
LN_EPS = 1e-5
NEG_INF = -1e30

GROUP = N_HEADS // N_KV_HEADS
ATTN_Q_DIM = N_HEADS * HEAD_DIM
KV_DIM = N_KV_HEADS * HEAD_DIM
MEM_DIM = N_MEM_HEADS * MEM_HEAD_DIM
GROUP_DIM = GROUP * HEAD_DIM
PROJ_DIM = 2 * CONV_DIM + ATTN_Q_DIM + 2 * KV_DIM + MEM_DIM

V7X_LANES = 128
V7X_SUBLANES = 8
V7X_VMEM_LIMIT_BYTES = 56 * 1024 * 1024

BF16 = jnp.bfloat16
F32 = jnp.float32


def _params(*semantics):
    return pltpu.CompilerParams(dimension_semantics=semantics,
                                vmem_limit_bytes=V7X_VMEM_LIMIT_BYTES)


def _resident(shape):
    nd = len(shape)
    return pl.BlockSpec(shape, lambda *_: (0,) * nd, pipeline_mode=pl.Buffered(1))


def _dot(a, b):
    return jnp.dot(a, b, preferred_element_type=F32)


def _layer_norm(v, g, b):
    mu = jnp.mean(v, axis=-1, keepdims=True)
    d = v - mu
    var = jnp.mean(d * d, axis=-1, keepdims=True)
    return d * lax.rsqrt(var + LN_EPS) * g + b


def _in_proj_kernel(x_ref, w_ref, u_ref, q_ref, k4_ref, v4_ref, qm_ref):
    xb = x_ref[...].astype(BF16)

    def proj(lo, width):
        return _dot(xb, w_ref[:, lo:lo + width])

    a = proj(0, CONV_DIM)
    g = proj(CONV_DIM, CONV_DIM)
    u_ref[...] = a * jax.nn.sigmoid(g)
    off = 2 * CONV_DIM
    q_ref[...] = proj(off, ATTN_Q_DIM).astype(BF16)
    off += ATTN_Q_DIM
    k = proj(off, KV_DIM)
    v = proj(off + KV_DIM, KV_DIM)
    off += 2 * KV_DIM
    qm_ref[...] = proj(off, MEM_DIM).astype(BF16)

    first_half = lax.broadcasted_iota(jnp.int32, k.shape, 1) < HEAD_DIM

    def tile_heads(t):
        swapped = pltpu.roll(t, HEAD_DIM, axis=1)
        h0 = jnp.where(first_half, t, swapped).astype(BF16)
        h1 = jnp.where(first_half, swapped, t).astype(BF16)
        return jnp.concatenate([h0, h0, h1, h1], axis=1)

    k4_ref[...] = tile_heads(k)
    v4_ref[...] = tile_heads(v)


def _in_proj(x2, w_proj, tm):
    t = x2.shape[0]
    row = lambda width: pl.BlockSpec((tm, width), lambda i: (i, 0))
    out_dims = (CONV_DIM, ATTN_Q_DIM, N_KV_HEADS * GROUP_DIM, N_KV_HEADS * GROUP_DIM, MEM_DIM)
    out_dtypes = (F32, BF16, BF16, BF16, BF16)
    return pl.pallas_call(
        _in_proj_kernel,
        grid=(t // tm,),
        in_specs=[row(D_MODEL), _resident(w_proj.shape)],
        out_specs=[row(d) for d in out_dims],
        out_shape=[jax.ShapeDtypeStruct((t, d), dt) for d, dt in zip(out_dims, out_dtypes)],
        compiler_params=_params("parallel"),
        name="in_proj",
    )(x2, w_proj)


CONV_HALO = CONV_WIDTH // 2
_CONV_TILES = 8
_CONV_STEP = 16


def _conv_kernel(u_ref, w_ref, b_ref, g_ref, beta_ref, c_ref, seq_ref, taps_ref):
    sub, n_tiles, _ = u_ref.shape
    for k in range(CONV_WIDTH):
        taps_ref[k] = jnp.broadcast_to(w_ref[k:k + 1, :], (sub, CONV_DIM))

    def to_phase_major(i, carry):
        a0 = pl.multiple_of(i * sub, sub)
        seq_ref[pl.ds(CONV_HALO + a0, sub)] = pltpu.einshape("rac->arc", u_ref[:, pl.ds(a0, sub), :])
        return carry

    lax.fori_loop(0, n_tiles // sub, to_phase_major, 0)
    phase = lax.broadcasted_iota(jnp.int32, (CONV_HALO, sub, CONV_DIM), 1)
    before = pltpu.roll(seq_ref[n_tiles:n_tiles + CONV_HALO], 1, axis=1)
    seq_ref[0:CONV_HALO] = jnp.where(phase == 0, 0.0, before)
    after = pltpu.roll(seq_ref[CONV_HALO:2 * CONV_HALO], sub - 1, axis=1)
    seq_ref[CONV_HALO + n_tiles:2 * CONV_HALO + n_tiles] = jnp.where(phase == sub - 1, 0.0, after)

    def body(i, carry):
        a0 = pl.multiple_of(i * _CONV_STEP, _CONV_STEP)
        done = []
        for part in range(_CONV_STEP // _CONV_TILES):
            acc = jnp.zeros((_CONV_TILES, sub, CONV_DIM), F32)
            for k in range(CONV_WIDTH):
                acc = acc + seq_ref[pl.ds(a0 + part * _CONV_TILES + k, _CONV_TILES)] * taps_ref[k]
            y = _layer_norm(acc + b_ref[...], g_ref[...], beta_ref[...])
            done.append(y * jax.nn.sigmoid(y))
        y = jnp.concatenate(done, axis=0)
        c_ref[:, pl.ds(a0, _CONV_STEP), :] = pltpu.einshape("arc->rac", y).astype(BF16)
        return carry

    lax.fori_loop(0, n_tiles // _CONV_STEP, body, 0)


def _conv(u, dw_w, dw_b, ln_g, ln_b, seq):
    t = u.shape[0]
    n_tiles = seq // V7X_SUBLANES
    vec = lambda: _resident((1, CONV_DIM))
    block = pl.BlockSpec((V7X_SUBLANES, n_tiles, CONV_DIM), lambda b: (b, 0, 0))
    c = pl.pallas_call(
        _conv_kernel,
        grid=(t // seq,),
        in_specs=[block, _resident(dw_w.shape), vec(), vec(), vec()],
        out_specs=block,
        out_shape=jax.ShapeDtypeStruct((t // n_tiles, n_tiles, CONV_DIM), BF16),
        scratch_shapes=[pltpu.VMEM((n_tiles + 2 * CONV_HALO, V7X_SUBLANES, CONV_DIM), F32),
                        pltpu.VMEM((CONV_WIDTH, V7X_SUBLANES, CONV_DIM), F32)],
        compiler_params=_params("parallel"),
        name="conv",
    )(u.reshape(t // n_tiles, n_tiles, CONV_DIM), dw_w, dw_b, ln_g, ln_b)
    return c.reshape(t, CONV_DIM)


def _win_attn_kernel(rb_ref, q_ref, k4_ref, v4_ref, bucket_ref, sink_ref, o_ref, bias_ref):
    n = pl.program_id(1)
    n_blocks = pl.num_programs(1)
    rows = GROUP * BLOCK
    keys = 3 * BLOCK

    @pl.when((pl.program_id(0) == 0) & (n == 0))
    def _():
        bucket = bucket_ref[...]
        qpos = lax.broadcasted_iota(jnp.int32, (BLOCK, keys), 0)
        kpos = lax.broadcasted_iota(jnp.int32, (BLOCK, keys), 1) - BLOCK
        in_band = jnp.abs(kpos - qpos) <= WINDOW
        for h in range(N_HEADS):
            acc = jnp.zeros((BLOCK, keys), F32)
            for bkt in range(N_BUCKETS):
                acc = jnp.where(bucket == bkt, rb_ref[bkt, h], acc)
            j, g = divmod(h, GROUP)
            bias_ref[j, g * BLOCK:(g + 1) * BLOCK, :] = jnp.where(in_band, acc, NEG_INF)

    starts = [pl.multiple_of(jnp.maximum(n - 1, 0) * BLOCK, BLOCK),
              pl.multiple_of(n * BLOCK, BLOCK),
              pl.multiple_of(jnp.minimum(n + 1, n_blocks - 1) * BLOCK, BLOCK)]
    col = lax.broadcasted_iota(jnp.int32, (rows, keys), 1)
    in_seq = ((col >= BLOCK) | (n > 0)) & ((col < 2 * BLOCK) | (n < n_blocks - 1))
    head_of_lane = lax.broadcasted_iota(jnp.int32, (BLOCK, GROUP_DIM), 1) // HEAD_DIM
    scale = HEAD_DIM ** -0.5

    outs = []
    for j in range(N_KV_HEADS):
        lanes = slice(j * GROUP_DIM, (j + 1) * GROUP_DIM)
        qg = q_ref[:, lanes]
        qm = jnp.concatenate([jnp.where(head_of_lane == g, qg, jnp.zeros_like(qg))
                              for g in range(GROUP)], axis=0)
        kwin = jnp.concatenate([k4_ref[pl.ds(s, BLOCK), lanes] for s in starts], axis=0)
        vwin = jnp.concatenate([v4_ref[pl.ds(s, BLOCK), lanes] for s in starts], axis=0)
        s = lax.dot_general(qm, kwin, (((1,), (1,)), ((), ())), preferred_element_type=F32)
        s = s * scale + bias_ref[j]
        s = jnp.where(in_seq, s, NEG_INF)
        sink = sink_ref[j]
        m = jnp.maximum(jnp.max(s, axis=-1, keepdims=True), sink)
        p = jnp.exp(s - m)
        denom = jnp.sum(p, axis=-1, keepdims=True) + jnp.exp(sink - m)
        pv = _dot(p.astype(BF16), vwin) * (1.0 / denom)
        og = jnp.zeros((BLOCK, GROUP_DIM), F32)
        for g in range(GROUP):
            og = jnp.where(head_of_lane == g, pv[g * BLOCK:(g + 1) * BLOCK, :], og)
        outs.append(og)
    o_ref[...] = jnp.concatenate(outs, axis=1).astype(BF16)


def _win_attn(q, k4, v4, rel_bias, bucket, sink_rows, seq):
    t = q.shape[0]
    n_blocks = seq // BLOCK
    rows = GROUP * BLOCK
    seq_spec = pl.BlockSpec((seq, N_KV_HEADS * GROUP_DIM), lambda b, n: (b, 0))
    return pl.pallas_call(
        _win_attn_kernel,
        grid=(t // seq, n_blocks),
        in_specs=[pl.BlockSpec(memory_space=pltpu.SMEM),
                  pl.BlockSpec((BLOCK, ATTN_Q_DIM), lambda b, n: (b * n_blocks + n, 0)),
                  seq_spec, seq_spec,
                  _resident(bucket.shape), _resident(sink_rows.shape)],
        out_specs=pl.BlockSpec((BLOCK, ATTN_Q_DIM), lambda b, n: (b * n_blocks + n, 0)),
        out_shape=jax.ShapeDtypeStruct((t, ATTN_Q_DIM), BF16),
        scratch_shapes=[pltpu.VMEM((N_KV_HEADS, rows, 3 * BLOCK), F32)],
        compiler_params=_params("arbitrary", "arbitrary"),
        name="win_attn",
    )(rel_bias, q, k4, v4, bucket, sink_rows)


def _mem_kv_kernel(m_ref, w_ref, km_ref, vm_ref):
    mb = m_ref[...].astype(BF16)
    km_ref[...] = _dot(mb, w_ref[:, :MEM_DIM]).astype(BF16)
    vm_ref[...] = _dot(mb, w_ref[:, MEM_DIM:]).astype(BF16)


def _mem_kv(mem2, w_mem_kv, tm):
    t = mem2.shape[0]
    out = pl.BlockSpec((tm, MEM_DIM), lambda i: (i, 0))
    return pl.pallas_call(
        _mem_kv_kernel,
        grid=(t // tm,),
        in_specs=[pl.BlockSpec((tm, D_MODEL), lambda i: (i, 0)), _resident(w_mem_kv.shape)],
        out_specs=[out, out],
        out_shape=[jax.ShapeDtypeStruct((t, MEM_DIM), BF16)] * 2,
        compiler_params=_params("parallel"),
        name="mem_kv",
    )(mem2, w_mem_kv)


def _merge_kernel(alpha, x_ref, c_ref, a_ref, qm_ref, km_ref, vm_ref, wg_ref, bg_ref,
                  wc_ref, wa_ref, wm_ref, wo_ref, g_ref, b_ref, h_ref):
    x = x_ref[...]
    xb = x.astype(BF16)

    mem_scale = MEM_HEAD_DIM ** -0.5
    heads = []
    for h in range(N_MEM_HEADS):
        lanes = slice(h * MEM_HEAD_DIM, (h + 1) * MEM_HEAD_DIM)
        s = lax.dot_general(qm_ref[:, lanes], km_ref[:, lanes], (((1,), (1,)), ((), ())),
                            preferred_element_type=F32) * mem_scale
        p = jnp.exp(s - jnp.max(s, axis=-1, keepdims=True))
        denom = jnp.sum(p, axis=-1, keepdims=True)
        heads.append(_dot(p.astype(BF16), vm_ref[:, lanes]) * (1.0 / denom))
    mo = jnp.concatenate(heads, axis=1).astype(BF16)

    def gate(i):
        cols = slice(i * D_MODEL, (i + 1) * D_MODEL)
        return jax.nn.sigmoid(_dot(xb, wg_ref[:, cols]) + bg_ref[:, cols])

    merged = gate(0) * _dot(c_ref[...], wc_ref[...])
    merged = merged + gate(1) * _dot(a_ref[...], wa_ref[...])
    merged = merged + gate(2) * _dot(mo, wm_ref[...])
    out = _dot(merged.astype(BF16), wo_ref[...])
    h_ref[...] = _layer_norm(alpha * x + out, g_ref[...], b_ref[...])


def _merge(alpha, x2, c, a, qm, km, vm, wg, bg, wc, wa, wm, wo, ln_g, ln_b, seq, tm):
    t = x2.shape[0]
    tiles_per_seq = seq // tm
    row = lambda width: pl.BlockSpec((tm, width), lambda i: (i, 0))
    mem_spec = pl.BlockSpec((N_MEM, MEM_DIM), lambda i: (i // tiles_per_seq, 0))
    weights = (wg, bg, wc, wa, wm, wo, ln_g, ln_b)
    return pl.pallas_call(
        functools.partial(_merge_kernel, alpha),
        grid=(t // tm,),
        in_specs=[row(D_MODEL), row(CONV_DIM), row(ATTN_Q_DIM), row(MEM_DIM), mem_spec, mem_spec]
                 + [_resident(w.shape) for w in weights],
        out_specs=row(D_MODEL),
        out_shape=jax.ShapeDtypeStruct((t, D_MODEL), F32),
        compiler_params=_params("parallel"),
        name="merge",
    )(x2, c, a, qm, km, vm, *weights)


def _ffn_kernel(alpha, chunk, h_ref, w1_ref, w2_ref, g_ref, b_ref, o_ref, act_ref):
    h = h_ref[...]
    hb = h.astype(BF16)
    d_ff = w2_ref.shape[0]
    for lo in range(0, d_ff, chunk):
        gate = _dot(hb, w1_ref[:, lo:lo + chunk])
        up = _dot(hb, w1_ref[:, d_ff + lo:d_ff + lo + chunk])
        act_ref[:, lo:lo + chunk] = (gate * jax.nn.sigmoid(gate) * up).astype(BF16)
    ffn = _dot(act_ref[...], w2_ref[...])
    o_ref[...] = _layer_norm(alpha * h + ffn, g_ref[...], b_ref[...])


def _ffn(alpha, h, w1, w2, ln_g, ln_b, tm, chunk):
    t = h.shape[0]
    d_ff = w2.shape[0]
    row = pl.BlockSpec((tm, D_MODEL), lambda i: (i, 0))
    weights = (w1, w2, ln_g, ln_b)
    return pl.pallas_call(
        functools.partial(_ffn_kernel, alpha, chunk),
        grid=(t // tm,),
        in_specs=[row] + [_resident(w.shape) for w in weights],
        out_specs=row,
        out_shape=jax.ShapeDtypeStruct((t, D_MODEL), F32),
        scratch_shapes=[pltpu.VMEM((tm, d_ff), BF16)],
        compiler_params=_params("parallel"),
        name="ffn",
    )(h, *weights)


def _t5_bucket(rel):
    half = N_BUCKETS // 2
    max_exact = half // 2
    base = jnp.where(rel > 0, half, 0)
    n = jnp.abs(rel)
    nf = jnp.maximum(n, 1).astype(jnp.float32)
    large = max_exact + (jnp.log(nf / max_exact) / math.log(MAX_DISTANCE / max_exact)
                         * (half - max_exact)).astype(jnp.int32)
    large = jnp.minimum(large, half - 1)
    return base + jnp.where(n < max_exact, n, large)


def _layer(x2, mem2, bucket, rel_bias, seq, alpha, w_in, b_gate, conv_dw_w, conv_dw_b, conv_ln_g,
           conv_ln_b, w_conv_out, attn_sink, w_attn_out, w_mem_kv, w_mem_out, w_o, ln1_g, ln1_b,
           w_ffn_in, w_ffn_out, ln2_g, ln2_b):
    row = lambda v: v.reshape(1, -1)
    w_in_b = w_in.astype(BF16)
    u, q, k4, v4, qm = _in_proj(x2, w_in_b[:, :PROJ_DIM], tm=512)
    c = _conv(u, conv_dw_w, row(conv_dw_b), row(conv_ln_g), row(conv_ln_b), seq)
    sink_rows = jnp.repeat(attn_sink.astype(F32).reshape(N_KV_HEADS, GROUP), BLOCK, axis=1)[..., None]
    a = _win_attn(q, k4, v4, rel_bias.astype(F32), bucket, sink_rows, seq)
    km, vm = _mem_kv(mem2, w_mem_kv.astype(BF16), tm=512)
    h = _merge(alpha, x2, c, a, qm, km, vm, w_in_b[:, PROJ_DIM:], row(b_gate),
               w_conv_out.astype(BF16), w_attn_out.astype(BF16), w_mem_out.astype(BF16),
               w_o.astype(BF16), row(ln1_g), row(ln1_b), seq, tm=512)
    return _ffn(alpha, h, w_ffn_in.astype(BF16), w_ffn_out.astype(BF16), row(ln2_g), row(ln2_b),
                tm=512, chunk=256)


def kernel(x, mem, rel_bias, w_in, b_gate, conv_dw_w, conv_dw_b, conv_ln_g, conv_ln_b, w_conv_out,
           attn_sink, w_attn_out, w_mem_kv, w_mem_out, w_o, ln1_g, ln1_b, w_ffn_in, w_ffn_out,
           ln2_g, ln2_b):
    batch, seq, d_model = x.shape
    depth = w_in.shape[0]
    assert d_model == D_MODEL and mem.shape[1:] == (N_MEM, D_MODEL) and seq % BLOCK == 0
    alpha = (2 * depth) ** 0.25
    qloc = jnp.arange(BLOCK, dtype=jnp.int32)
    kloc = jnp.arange(3 * BLOCK, dtype=jnp.int32) - BLOCK
    bucket = _t5_bucket(kloc[None, :] - qloc[:, None])
    x2 = x.reshape(batch * seq, d_model)
    mem2 = mem.reshape(batch * N_MEM, d_model)
    per_layer = (w_in, b_gate, conv_dw_w, conv_dw_b, conv_ln_g, conv_ln_b, w_conv_out, attn_sink,
                 w_attn_out, w_mem_kv, w_mem_out, w_o, ln1_g, ln1_b, w_ffn_in, w_ffn_out, ln2_g, ln2_b)
    for l in range(depth):
        x2 = _layer(x2, mem2, bucket, rel_bias, seq, alpha, *(p[l] for p in per_layer))
    return x2.reshape(batch, seq, d_model)
```

```python
import functools
import math

import jax
import jax.numpy as jnp
from jax import lax
from jax.experimental import pallas as pl
from jax.experimental.pallas import tpu as pltpu

D_MODEL = 1024
N_MEM = 256
CONV_DIM = 512
CONV_WIDTH = 31
N_HEADS = 8
N_KV_HEADS = 2
HEAD_DIM = 64
WINDOW = 128
BLOCK = 128
N_MEM_HEADS = 4
MEM_HEAD_DIM = 128
N_BUCKETS = 32
MAX_DISTANCE = 128
N_BRANCHES = 3
LN_EPS = 1e-5
NEG_INF = -1e30

GROUP = N_HEADS // N_KV_HEADS
ATTN_Q_DIM = N_HEADS * HEAD_DIM
KV_DIM = N_KV_HEADS * HEAD_DIM
MEM_DIM = N_MEM_HEADS * MEM_HEAD_DIM
GROUP_DIM = GROUP * HEAD_DIM
ATTN_SCALE = HEAD_DIM ** -0.5
assert math.frexp(ATTN_SCALE)[0] == 0.5, "q is pre-scaled before rounding: needs a power-of-two scale"
PROJ_DIM = 2 * CONV_DIM + ATTN_Q_DIM + 2 * KV_DIM + MEM_DIM

V7X_LANES = 128
V7X_SUBLANES = 8
V7X_VMEM_LIMIT_BYTES = 56 * 1024 * 1024

BF16 = jnp.bfloat16
F32 = jnp.float32


def _params(*semantics):
    return pltpu.CompilerParams(dimension_semantics=semantics,
                                vmem_limit_bytes=V7X_VMEM_LIMIT_BYTES)


def _resident(shape):
    nd = len(shape)
    return pl.BlockSpec(shape, lambda *_: (0,) * nd, pipeline_mode=pl.Buffered(1))


def _dot(a, b):
    return jnp.dot(a, b, preferred_element_type=F32)


def _layer_norm(v, g, b):
    mu = jnp.mean(v, axis=-1, keepdims=True)
    d = v - mu
    var = jnp.mean(d * d, axis=-1, keepdims=True)
    return d * lax.rsqrt(var + LN_EPS) * g + b


def _in_proj_kernel(x_ref, w_ref, u_ref, q_ref, k4_ref, v4t_ref, qm_ref):
    xb = x_ref[...].astype(BF16)

    def proj(lo, width):
        return _dot(xb, w_ref[:, lo:lo + width])

    a = proj(0, CONV_DIM)
    g = proj(CONV_DIM, CONV_DIM)
    u_ref[...] = a * jax.nn.sigmoid(g)
    off = 2 * CONV_DIM
    q_ref[...] = (proj(off, ATTN_Q_DIM) * ATTN_SCALE).astype(BF16)
    off += ATTN_Q_DIM
    k = proj(off, KV_DIM)
    v = proj(off + KV_DIM, KV_DIM)
    off += 2 * KV_DIM
    qm_ref[...] = proj(off, MEM_DIM).astype(BF16)

    first_half = lax.broadcasted_iota(jnp.int32, k.shape, 1) < HEAD_DIM
    swapped = pltpu.roll(k, HEAD_DIM, axis=1)
    h0 = jnp.where(first_half, k, swapped).astype(BF16)
    h1 = jnp.where(first_half, swapped, k).astype(BF16)
    k4_ref[...] = jnp.concatenate([h0, h0, h1, h1], axis=1)

    vt = v.T.astype(BF16)
    for c in range(v4t_ref.shape[0]):
        keys = vt[:, c * BLOCK:(c + 1) * BLOCK]
        v4t_ref[c] = jnp.concatenate([keys[:HEAD_DIM]] * GROUP + [keys[HEAD_DIM:]] * GROUP, axis=0)


def _in_proj(x2, w_proj, tm):
    t = x2.shape[0]
    row = lambda width: pl.BlockSpec((tm, width), lambda i: (i, 0))
    kv_tiled = N_KV_HEADS * GROUP_DIM
    out_specs = [row(CONV_DIM), row(ATTN_Q_DIM), row(kv_tiled),
                 pl.BlockSpec((tm // BLOCK, kv_tiled, BLOCK), lambda i: (i, 0, 0)), row(MEM_DIM)]
    out_shape = [jax.ShapeDtypeStruct((t, CONV_DIM), F32),
                 jax.ShapeDtypeStruct((t, ATTN_Q_DIM), BF16),
                 jax.ShapeDtypeStruct((t, kv_tiled), BF16),
                 jax.ShapeDtypeStruct((t // BLOCK, kv_tiled, BLOCK), BF16),
                 jax.ShapeDtypeStruct((t, MEM_DIM), BF16)]
    return pl.pallas_call(
        _in_proj_kernel,
        grid=(t // tm,),
        in_specs=[row(D_MODEL), _resident(w_proj.shape)],
        out_specs=out_specs,
        out_shape=out_shape,
        compiler_params=_params("parallel"),
        name="in_proj",
    )(x2, w_proj)


CONV_HALO = CONV_WIDTH // 2
_CONV_TILES = 8
_CONV_STEP = 16


def _conv_kernel(u_ref, w_ref, b_ref, g_ref, beta_ref, c_ref, seq_ref, taps_ref):
    sub, n_tiles, _ = u_ref.shape
    for k in range(CONV_WIDTH):
        taps_ref[k] = jnp.broadcast_to(w_ref[k:k + 1, :], (sub, CONV_DIM))

    def to_phase_major(i, carry):
        a0 = pl.multiple_of(i * sub, sub)
        seq_ref[pl.ds(CONV_HALO + a0, sub)] = pltpu.einshape("rac->arc", u_ref[:, pl.ds(a0, sub), :])
        return carry

    lax.fori_loop(0, n_tiles // sub, to_phase_major, 0)
    phase = lax.broadcasted_iota(jnp.int32, (CONV_HALO, sub, CONV_DIM), 1)
    before = pltpu.roll(seq_ref[n_tiles:n_tiles + CONV_HALO], 1, axis=1)
    seq_ref[0:CONV_HALO] = jnp.where(phase == 0, 0.0, before)
    after = pltpu.roll(seq_ref[CONV_HALO:2 * CONV_HALO], sub - 1, axis=1)
    seq_ref[CONV_HALO + n_tiles:2 * CONV_HALO + n_tiles] = jnp.where(phase == sub - 1, 0.0, after)

    def body(i, carry):
        a0 = pl.multiple_of(i * _CONV_STEP, _CONV_STEP)
        done = []
        for part in range(_CONV_STEP // _CONV_TILES):
            acc = jnp.zeros((_CONV_TILES, sub, CONV_DIM), F32)
            for k in range(CONV_WIDTH):
                acc = acc + seq_ref[pl.ds(a0 + part * _CONV_TILES + k, _CONV_TILES)] * taps_ref[k]
            y = _layer_norm(acc + b_ref[...], g_ref[...], beta_ref[...])
            done.append(y * jax.nn.sigmoid(y))
        y = jnp.concatenate(done, axis=0)
        c_ref[:, pl.ds(a0, _CONV_STEP), :] = pltpu.einshape("arc->rac", y).astype(BF16)
        return carry

    lax.fori_loop(0, n_tiles // _CONV_STEP, body, 0)


def _conv(u, dw_w, dw_b, ln_g, ln_b, seq):
    t = u.shape[0]
    n_tiles = seq // V7X_SUBLANES
    vec = lambda: _resident((1, CONV_DIM))
    block = pl.BlockSpec((V7X_SUBLANES, n_tiles, CONV_DIM), lambda b: (b, 0, 0))
    c = pl.pallas_call(
        _conv_kernel,
        grid=(t // seq,),
        in_specs=[block, _resident(dw_w.shape), vec(), vec(), vec()],
        out_specs=block,
        out_shape=jax.ShapeDtypeStruct((t // n_tiles, n_tiles, CONV_DIM), BF16),
        scratch_shapes=[pltpu.VMEM((n_tiles + 2 * CONV_HALO, V7X_SUBLANES, CONV_DIM), F32),
                        pltpu.VMEM((CONV_WIDTH, V7X_SUBLANES, CONV_DIM), F32)],
        compiler_params=_params("parallel"),
        name="conv",
    )(u.reshape(t // n_tiles, n_tiles, CONV_DIM), dw_w, dw_b, ln_g, ln_b)
    return c.reshape(t, CONV_DIM)


def _win_attn_kernel(rb_ref, q_ref, k4_ref, v4t_ref, bucket_ref, sink_ref, o_ref, bias_ref):
    n = pl.program_id(1)
    last = pl.num_programs(1) - 1
    keys = 3 * BLOCK

    @pl.when((pl.program_id(0) == 0) & (n == 0))
    def _():
        bucket = bucket_ref[...]
        kpos = lax.broadcasted_iota(jnp.int32, (keys, BLOCK), 0) - BLOCK
        qpos = lax.broadcasted_iota(jnp.int32, (keys, BLOCK), 1)
        in_band = jnp.abs(kpos - qpos) <= WINDOW
        for h in range(N_HEADS):
            acc = jnp.zeros((keys, BLOCK), F32)
            for bkt in range(N_BUCKETS):
                acc = jnp.where(bucket == bkt, rb_ref[bkt, h], acc)
            j, g = divmod(h, GROUP)
            bias_ref[j, :, g * BLOCK:(g + 1) * BLOCK] = jnp.where(in_band, acc, NEG_INF)

    blocks = [jnp.maximum(n - 1, 0), n, jnp.minimum(n + 1, last)]
    head_of_lane = lax.broadcasted_iota(jnp.int32, (BLOCK, GROUP_DIM), 1) // HEAD_DIM

    heads = []
    for j in range(N_KV_HEADS):
        lanes = slice(j * GROUP_DIM, (j + 1) * GROUP_DIM)
        qg = q_ref[:, lanes]
        qm = jnp.concatenate([jnp.where(head_of_lane == g, qg, jnp.zeros_like(qg))
                              for g in range(GROUP)], axis=0)
        kwin = jnp.concatenate([k4_ref[pl.ds(pl.multiple_of(b * BLOCK, BLOCK), BLOCK), lanes]
                                for b in blocks], axis=0)
        vwin = jnp.concatenate([v4t_ref[b, lanes, :] for b in blocks], axis=1)
        s = lax.dot_general(kwin, qm, (((1,), (1,)), ((), ())), preferred_element_type=F32)
        s = s + bias_ref[j]
        s = jnp.concatenate([jnp.where(n > 0, s[:BLOCK], NEG_INF), s[BLOCK:2 * BLOCK],
                             jnp.where(n < last, s[2 * BLOCK:], NEG_INF)], axis=0)
        sink = sink_ref[j]
        m = jnp.maximum(jnp.max(s, axis=0, keepdims=True), sink)
        p = jnp.exp(s - m)
        denom = jnp.sum(p, axis=0, keepdims=True) + jnp.exp(sink - m)
        pv = _dot(vwin, p.astype(BF16))
        inv = 1.0 / denom
        for g in range(GROUP):
            cols = slice(g * BLOCK, (g + 1) * BLOCK)
            heads.append(pv[g * HEAD_DIM:(g + 1) * HEAD_DIM, cols] * inv[:, cols])
    o_ref[...] = jnp.concatenate(heads, axis=0).T.astype(BF16)


def _win_attn(q, k4, v4t, rel_bias, bucket_t, sink_rows, seq):
    t = q.shape[0]
    n_blocks = seq // BLOCK
    kv_tiled = N_KV_HEADS * GROUP_DIM
    return pl.pallas_call(
        _win_attn_kernel,
        grid=(t // seq, n_blocks),
        in_specs=[pl.BlockSpec(memory_space=pltpu.SMEM),
                  pl.BlockSpec((BLOCK, ATTN_Q_DIM), lambda b, n: (b * n_blocks + n, 0)),
                  pl.BlockSpec((seq, kv_tiled), lambda b, n: (b, 0)),
                  pl.BlockSpec((n_blocks, kv_tiled, BLOCK), lambda b, n: (b, 0, 0)),
                  _resident(bucket_t.shape), _resident(sink_rows.shape)],
        out_specs=pl.BlockSpec((BLOCK, ATTN_Q_DIM), lambda b, n: (b * n_blocks + n, 0)),
        out_shape=jax.ShapeDtypeStruct((t, ATTN_Q_DIM), BF16),
        scratch_shapes=[pltpu.VMEM((N_KV_HEADS, 3 * BLOCK, GROUP * BLOCK), F32)],
        compiler_params=_params("arbitrary", "arbitrary"),
        name="win_attn",
    )(rel_bias, q, k4, v4t, bucket_t, sink_rows)


def _mem_kv_kernel(m_ref, w_ref, km_ref, vm_ref):
    mb = m_ref[...].astype(BF16)
    km_ref[...] = _dot(mb, w_ref[:, :MEM_DIM]).astype(BF16)
    vm_ref[...] = _dot(mb, w_ref[:, MEM_DIM:]).astype(BF16)


def _mem_kv(mem2, w_mem_kv, tm):
    t = mem2.shape[0]
    out = pl.BlockSpec((tm, MEM_DIM), lambda i: (i, 0))
    return pl.pallas_call(
        _mem_kv_kernel,
        grid=(t // tm,),
        in_specs=[pl.BlockSpec((tm, D_MODEL), lambda i: (i, 0)), _resident(w_mem_kv.shape)],
        out_specs=[out, out],
        out_shape=[jax.ShapeDtypeStruct((t, MEM_DIM), BF16)] * 2,
        compiler_params=_params("parallel"),
        name="mem_kv",
    )(mem2, w_mem_kv)


def _merge_kernel(alpha, x_ref, c_ref, a_ref, qm_ref, km_ref, vm_ref, wg_ref, bg_ref,
                  wc_ref, wa_ref, wm_ref, wo_ref, g_ref, b_ref, h_ref):
    x = x_ref[...]
    xb = x.astype(BF16)

    mem_scale = MEM_HEAD_DIM ** -0.5
    heads = []
    for h in range(N_MEM_HEADS):
        lanes = slice(h * MEM_HEAD_DIM, (h + 1) * MEM_HEAD_DIM)
        s = lax.dot_general(qm_ref[:, lanes], km_ref[:, lanes], (((1,), (1,)), ((), ())),
                            preferred_element_type=F32) * mem_scale
        p = jnp.exp(s - jnp.max(s, axis=-1, keepdims=True))
        denom = jnp.sum(p, axis=-1, keepdims=True)
        heads.append(_dot(p.astype(BF16), vm_ref[:, lanes]) * (1.0 / denom))
    mo = jnp.concatenate(heads, axis=1).astype(BF16)

    def gate(i):
        cols = slice(i * D_MODEL, (i + 1) * D_MODEL)
        return jax.nn.sigmoid(_dot(xb, wg_ref[:, cols]) + bg_ref[:, cols])

    merged = gate(0) * _dot(c_ref[...], wc_ref[...])
    merged = merged + gate(1) * _dot(a_ref[...], wa_ref[...])
    merged = merged + gate(2) * _dot(mo, wm_ref[...])
    out = _dot(merged.astype(BF16), wo_ref[...])
    h_ref[...] = _layer_norm(alpha * x + out, g_ref[...], b_ref[...])


def _merge(alpha, x2, c, a, qm, km, vm, wg, bg, wc, wa, wm, wo, ln_g, ln_b, seq, tm):
    t = x2.shape[0]
    tiles_per_seq = seq // tm
    row = lambda width: pl.BlockSpec((tm, width), lambda i: (i, 0))
    mem_spec = pl.BlockSpec((N_MEM, MEM_DIM), lambda i: (i // tiles_per_seq, 0))
    weights = (wg, bg, wc, wa, wm, wo, ln_g, ln_b)
    return pl.pallas_call(
        functools.partial(_merge_kernel, alpha),
        grid=(t // tm,),
        in_specs=[row(D_MODEL), row(CONV_DIM), row(ATTN_Q_DIM), row(MEM_DIM), mem_spec, mem_spec]
                 + [_resident(w.shape) for w in weights],
        out_specs=row(D_MODEL),
        out_shape=jax.ShapeDtypeStruct((t, D_MODEL), F32),
        compiler_params=_params("parallel"),
        name="merge",
    )(x2, c, a, qm, km, vm, *weights)


def _ffn_kernel(alpha, chunk, h_ref, w1_ref, w2_ref, g_ref, b_ref, o_ref, act_ref):
    h = h_ref[...]
    hb = h.astype(BF16)
    d_ff = w2_ref.shape[0]
    for lo in range(0, d_ff, chunk):
        gate = _dot(hb, w1_ref[:, lo:lo + chunk])
        up = _dot(hb, w1_ref[:, d_ff + lo:d_ff + lo + chunk])
        act_ref[:, lo:lo + chunk] = (gate * jax.nn.sigmoid(gate) * up).astype(BF16)
    ffn = _dot(act_ref[...], w2_ref[...])
    o_ref[...] = _layer_norm(alpha * h + ffn, g_ref[...], b_ref[...])


def _ffn(alpha, h, w1, w2, ln_g, ln_b, tm, chunk):
    t = h.shape[0]
    d_ff = w2.shape[0]
    row = pl.BlockSpec((tm, D_MODEL), lambda i: (i, 0))
    weights = (w1, w2, ln_g, ln_b)
    return pl.pallas_call(
        functools.partial(_ffn_kernel, alpha, chunk),
        grid=(t // tm,),
        in_specs=[row] + [_resident(w.shape) for w in weights],
        out_specs=row,
        out_shape=jax.ShapeDtypeStruct((t, D_MODEL), F32),
        scratch_shapes=[pltpu.VMEM((tm, d_ff), BF16)],
        compiler_params=_params("parallel"),
        name="ffn",
    )(h, *weights)


def _t5_bucket(rel):
    half = N_BUCKETS // 2
    max_exact = half // 2
    base = jnp.where(rel > 0, half, 0)
    n = jnp.abs(rel)
    nf = jnp.maximum(n, 1).astype(jnp.float32)
    large = max_exact + (jnp.log(nf / max_exact) / math.log(MAX_DISTANCE / max_exact)
                         * (half - max_exact)).astype(jnp.int32)
    large = jnp.minimum(large, half - 1)
    return base + jnp.where(n < max_exact, n, large)


def _layer(x2, mem2, bucket, rel_bias, seq, alpha, w_in, b_gate, conv_dw_w, conv_dw_b, conv_ln_g,
           conv_ln_b, w_conv_out, attn_sink, w_attn_out, w_mem_kv, w_mem_out, w_o, ln1_g, ln1_b,
           w_ffn_in, w_ffn_out, ln2_g, ln2_b):
    row = lambda v: v.reshape(1, -1)
    w_in_b = w_in.astype(BF16)
    u, q, k4, v4t, qm = _in_proj(x2, w_in_b[:, :PROJ_DIM], tm=512)
    c = _conv(u, conv_dw_w, row(conv_dw_b), row(conv_ln_g), row(conv_ln_b), seq)
    sink_rows = jnp.repeat(attn_sink.astype(F32).reshape(N_KV_HEADS, GROUP), BLOCK, axis=1)[:, None, :]
    a = _win_attn(q, k4, v4t, rel_bias.astype(F32), bucket, sink_rows, seq)
    km, vm = _mem_kv(mem2, w_mem_kv.astype(BF16), tm=512)
    h = _merge(alpha, x2, c, a, qm, km, vm, w_in_b[:, PROJ_DIM:], row(b_gate),
               w_conv_out.astype(BF16), w_attn_out.astype(BF16), w_mem_out.astype(BF16),
               w_o.astype(BF16), row(ln1_g), row(ln1_b), seq, tm=512)
    return _ffn(alpha, h, w_ffn_in.astype(BF16), w_ffn_out.astype(BF16), row(ln2_g), row(ln2_b),
                tm=512, chunk=256)


def kernel(x, mem, rel_bias, w_in, b_gate, conv_dw_w, conv_dw_b, conv_ln_g, conv_ln_b, w_conv_out,
           attn_sink, w_attn_out, w_mem_kv, w_mem_out, w_o, ln1_g, ln1_b, w_ffn_in, w_ffn_out,
           ln2_g, ln2_b):
    batch, seq, d_model = x.shape
    depth = w_in.shape[0]
    assert d_model == D_MODEL and mem.shape[1:] == (N_MEM, D_MODEL) and seq % BLOCK == 0
    alpha = (2 * depth) ** 0.25
    qloc = jnp.arange(BLOCK, dtype=jnp.int32)
    kloc = jnp.arange(3 * BLOCK, dtype=jnp.int32) - BLOCK
    bucket = _t5_bucket(kloc[:, None] - qloc[None, :])
    x2 = x.reshape(batch * seq, d_model)
    mem2 = mem.reshape(batch * N_MEM, d_model)
    per_layer = (w_in, b_gate, conv_dw_w, conv_dw_b, conv_ln_g, conv_ln_b, w_conv_out, attn_sink,
                 w_attn_out, w_mem_kv, w_mem_out, w_o, ln1_g, ln1_b, w_ffn_in, w_ffn_out, ln2_g, ln2_b)
    for l in range(depth):
        x2 = _layer(x2, mem2, bucket, rel_bias, seq, alpha, *(p[l] for p in per_layer))
    return x2.reshape(batch, seq, d_model)
```

```python
import functools
import math

import jax
import jax.numpy as jnp
from jax import lax
from jax.experimental import pallas as pl
from jax.experimental.pallas import tpu as pltpu

D_MODEL = 1024
N_MEM = 256
CONV_DIM = 512
CONV_WIDTH = 31
N_HEADS = 8
N_KV_HEADS = 2
HEAD_DIM = 64
WINDOW = 128
BLOCK = 128
N_MEM_HEADS = 4
MEM_HEAD_DIM = 128
N_BUCKETS = 32
MAX_DISTANCE = 128
N_BRANCHES = 3
LN_EPS = 1e-5
NEG_INF = -1e30

GROUP = N_HEADS // N_KV_HEADS
ATTN_Q_DIM = N_HEADS * HEAD_DIM
KV_DIM = N_KV_HEADS * HEAD_DIM
MEM_DIM = N_MEM_HEADS * MEM_HEAD_DIM
GROUP_DIM = GROUP * HEAD_DIM
ATTN_SCALE = HEAD_DIM ** -0.5
LOG2E = math.log2(math.e)
PROJ_DIM = 2 * CONV_DIM + ATTN_Q_DIM + 2 * KV_DIM + MEM_DIM

V7X_LANES = 128
V7X_SUBLANES = 8
V7X_VMEM_LIMIT_BYTES = 56 * 1024 * 1024

BF16 = jnp.bfloat16
F32 = jnp.float32


def _params(*semantics):
    return pltpu.CompilerParams(dimension_semantics=semantics,
                                vmem_limit_bytes=V7X_VMEM_LIMIT_BYTES)


def _resident(shape):
    nd = len(shape)
    return pl.BlockSpec(shape, lambda *_: (0,) * nd, pipeline_mode=pl.Buffered(1))


def _dot(a, b):
    return jnp.dot(a, b, preferred_element_type=F32)


def _layer_norm(v, g, b):
    mu = jnp.mean(v, axis=-1, keepdims=True)
    d = v - mu
    var = jnp.mean(d * d, axis=-1, keepdims=True)
    return d * lax.rsqrt(var + LN_EPS) * g + b


def _in_proj_kernel(x_ref, w_ref, u_ref, q_ref, k4_ref, v4t_ref, qm_ref):
    xb = x_ref[...].astype(BF16)

    def proj(lo, width):
        return _dot(xb, w_ref[:, lo:lo + width])

    a = proj(0, CONV_DIM)
    g = proj(CONV_DIM, CONV_DIM)
    u_ref[...] = a * jax.nn.sigmoid(g)
    off = 2 * CONV_DIM
    q_ref[...] = (proj(off, ATTN_Q_DIM) * (ATTN_SCALE * LOG2E)).astype(BF16)
    off += ATTN_Q_DIM
    k = proj(off, KV_DIM)
    v = proj(off + KV_DIM, KV_DIM)
    off += 2 * KV_DIM
    qm_ref[...] = proj(off, MEM_DIM).astype(BF16)

    first_half = lax.broadcasted_iota(jnp.int32, k.shape, 1) < HEAD_DIM
    swapped = pltpu.roll(k, HEAD_DIM, axis=1)
    h0 = jnp.where(first_half, k, swapped).astype(BF16)
    h1 = jnp.where(first_half, swapped, k).astype(BF16)
    k4_ref[...] = jnp.concatenate([h0, h0, h1, h1], axis=1)

    vt = v.T.astype(BF16)
    for c in range(v4t_ref.shape[0]):
        keys = vt[:, c * BLOCK:(c + 1) * BLOCK]
        v4t_ref[c] = jnp.concatenate([keys[:HEAD_DIM]] * GROUP + [keys[HEAD_DIM:]] * GROUP, axis=0)


def _in_proj(x2, w_proj, tm):
    t = x2.shape[0]
    row = lambda width: pl.BlockSpec((tm, width), lambda i: (i, 0))
    kv_tiled = N_KV_HEADS * GROUP_DIM
    out_specs = [row(CONV_DIM), row(ATTN_Q_DIM), row(kv_tiled),
                 pl.BlockSpec((tm // BLOCK, kv_tiled, BLOCK), lambda i: (i, 0, 0)), row(MEM_DIM)]
    out_shape = [jax.ShapeDtypeStruct((t, CONV_DIM), F32),
                 jax.ShapeDtypeStruct((t, ATTN_Q_DIM), BF16),
                 jax.ShapeDtypeStruct((t, kv_tiled), BF16),
                 jax.ShapeDtypeStruct((t // BLOCK, kv_tiled, BLOCK), BF16),
                 jax.ShapeDtypeStruct((t, MEM_DIM), BF16)]
    return pl.pallas_call(
        _in_proj_kernel,
        grid=(t // tm,),
        in_specs=[row(D_MODEL), _resident(w_proj.shape)],
        out_specs=out_specs,
        out_shape=out_shape,
        compiler_params=_params("parallel"),
        name="in_proj",
    )(x2, w_proj)


CONV_HALO = CONV_WIDTH // 2
_CONV_TILES = 8
_CONV_STEP = 16


def _conv_kernel(u_ref, w_ref, b_ref, g_ref, beta_ref, c_ref, seq_ref, taps_ref):
    sub, n_tiles, _ = u_ref.shape
    for k in range(CONV_WIDTH):
        taps_ref[k] = jnp.broadcast_to(w_ref[k:k + 1, :], (sub, CONV_DIM))

    def to_phase_major(i, carry):
        a0 = pl.multiple_of(i * sub, sub)
        seq_ref[pl.ds(CONV_HALO + a0, sub)] = pltpu.einshape("rac->arc", u_ref[:, pl.ds(a0, sub), :])
        return carry

    lax.fori_loop(0, n_tiles // sub, to_phase_major, 0)
    phase = lax.broadcasted_iota(jnp.int32, (CONV_HALO, sub, CONV_DIM), 1)
    before = pltpu.roll(seq_ref[n_tiles:n_tiles + CONV_HALO], 1, axis=1)
    seq_ref[0:CONV_HALO] = jnp.where(phase == 0, 0.0, before)
    after = pltpu.roll(seq_ref[CONV_HALO:2 * CONV_HALO], sub - 1, axis=1)
    seq_ref[CONV_HALO + n_tiles:2 * CONV_HALO + n_tiles] = jnp.where(phase == sub - 1, 0.0, after)

    def body(i, carry):
        a0 = pl.multiple_of(i * _CONV_STEP, _CONV_STEP)
        done = []
        for part in range(_CONV_STEP // _CONV_TILES):
            acc = jnp.zeros((_CONV_TILES, sub, CONV_DIM), F32)
            for k in range(CONV_WIDTH):
                acc = acc + seq_ref[pl.ds(a0 + part * _CONV_TILES + k, _CONV_TILES)] * taps_ref[k]
            y = _layer_norm(acc + b_ref[...], g_ref[...], beta_ref[...])
            done.append(y * jax.nn.sigmoid(y))
        y = jnp.concatenate(done, axis=0)
        c_ref[:, pl.ds(a0, _CONV_STEP), :] = pltpu.einshape("arc->rac", y).astype(BF16)
        return carry

    lax.fori_loop(0, n_tiles // _CONV_STEP, body, 0)


def _conv(u, dw_w, dw_b, ln_g, ln_b, seq):
    t = u.shape[0]
    n_tiles = seq // V7X_SUBLANES
    vec = lambda: _resident((1, CONV_DIM))
    block = pl.BlockSpec((V7X_SUBLANES, n_tiles, CONV_DIM), lambda b: (b, 0, 0))
    c = pl.pallas_call(
        _conv_kernel,
        grid=(t // seq,),
        in_specs=[block, _resident(dw_w.shape), vec(), vec(), vec()],
        out_specs=block,
        out_shape=jax.ShapeDtypeStruct((t // n_tiles, n_tiles, CONV_DIM), BF16),
        scratch_shapes=[pltpu.VMEM((n_tiles + 2 * CONV_HALO, V7X_SUBLANES, CONV_DIM), F32),
                        pltpu.VMEM((CONV_WIDTH, V7X_SUBLANES, CONV_DIM), F32)],
        compiler_params=_params("parallel"),
        name="conv",
    )(u.reshape(t // n_tiles, n_tiles, CONV_DIM), dw_w, dw_b, ln_g, ln_b)
    return c.reshape(t, CONV_DIM)


def _win_attn_kernel(rb_ref, q_ref, k4_ref, v4t_ref, bucket_ref, sink_ref, o_ref,
                     bias_ref, s0_ref, s1_ref, p0_ref, p1_ref, inv0_ref, inv1_ref):
    s_ref, p_ref, inv_ref = (s0_ref, s1_ref), (p0_ref, p1_ref), (inv0_ref, inv1_ref)
    n_blocks = v4t_ref.shape[0]
    last = n_blocks - 1
    keys = 3 * BLOCK
    first_var, mid_var, last_var = 0, 1, 2

    @pl.when(pl.program_id(0) == 0)
    def _():
        bucket = bucket_ref[...]
        kpos = lax.broadcasted_iota(jnp.int32, (keys, BLOCK), 0) - BLOCK
        qpos = lax.broadcasted_iota(jnp.int32, (keys, BLOCK), 1)
        in_band = jnp.abs(kpos - qpos) <= WINDOW
        for h in range(N_HEADS):
            acc = jnp.zeros((keys, BLOCK), F32)
            for bkt in range(N_BUCKETS):
                acc = jnp.where(bucket == bkt, rb_ref[bkt, h] * LOG2E, acc)
            acc = jnp.where(in_band, acc, NEG_INF)
            j, g = divmod(h, GROUP)
            cols = slice(g * BLOCK, (g + 1) * BLOCK)
            bias_ref[mid_var, j, :, cols] = acc
            bias_ref[first_var, j, :, cols] = jnp.where(kpos >= 0, acc, NEG_INF)
            bias_ref[last_var, j, :, cols] = jnp.where(kpos < BLOCK, acc, NEG_INF)

    head_of_lane = lax.broadcasted_iota(jnp.int32, (BLOCK, GROUP_DIM), 1) // HEAD_DIM

    def window(n):
        return [jnp.maximum(n - 1, 0), n, jnp.minimum(n + 1, last)]

    def rows_of(n):
        return pl.ds(pl.multiple_of(n * BLOCK, BLOCK), BLOCK)

    halves = [(j, h) for j in range(N_KV_HEADS) for h in range(2)]
    half_cols = GROUP * BLOCK // 2

    def scores_chunks(n, slot):
        def chunk(j, h):
            lanes = slice(j * GROUP_DIM, (j + 1) * GROUP_DIM)
            qg = q_ref[rows_of(n), lanes]
            qm = jnp.concatenate([jnp.where(head_of_lane == g, qg, jnp.zeros_like(qg))
                                  for g in (2 * h, 2 * h + 1)], axis=0)
            kwin = jnp.concatenate([k4_ref[rows_of(b), lanes] for b in window(n)], axis=0)
            s_ref[slot][j, :, h * half_cols:(h + 1) * half_cols] = lax.dot_general(
                kwin, qm, (((1,), (1,)), ((), ())), preferred_element_type=F32)
        return [functools.partial(chunk, j, h) for j, h in halves]

    def softmax_chunks(slot, var):
        def chunk(j, h):
            for g in (2 * h, 2 * h + 1):
                cols = slice(g * BLOCK, (g + 1) * BLOCK)
                s = s_ref[slot][j, :, cols] + bias_ref[var, j, :, cols]
                sink = sink_ref[j, :, cols] * LOG2E
                m = jnp.maximum(jnp.max(s, axis=0, keepdims=True), sink)
                p = jnp.exp2(s - m)
                denom = jnp.sum(p, axis=0, keepdims=True) + jnp.exp2(sink - m)
                p_ref[slot][j, :, cols] = p.astype(BF16)
                inv_ref[slot][j, :, cols] = 1.0 / denom
        return [functools.partial(chunk, j, h) for j, h in halves]

    def values_chunks(n, slot):
        heads = []

        def chunk(j, h):
            lanes = slice(j * GROUP_DIM, (j + 1) * GROUP_DIM)
            vwin = jnp.concatenate([v4t_ref[b, lanes, :] for b in window(n)], axis=1)
            pv = _dot(vwin, p_ref[slot][j, :, h * half_cols:(h + 1) * half_cols])
            for i, g in enumerate((2 * h, 2 * h + 1)):
                inv = inv_ref[slot][j, :, g * BLOCK:(g + 1) * BLOCK]
                heads.append(pv[g * HEAD_DIM:(g + 1) * HEAD_DIM, i * BLOCK:(i + 1) * BLOCK] * inv)
            if len(heads) == N_HEADS:
                o_ref[rows_of(n), :] = jnp.concatenate(heads, axis=0).T.astype(BF16)
        return [functools.partial(chunk, j, h) for j, h in halves]

    def issue(*stages):
        for chunks in zip(*stages):
            for chunk in chunks:
                chunk()

    def two_steps(i, carry):
        for slot in range(2):
            t = 2 * i + slot
            issue(values_chunks(t - 2, slot), softmax_chunks(1 - slot, mid_var), scores_chunks(t, slot))
        return carry

    issue(scores_chunks(0, 0))
    issue(softmax_chunks(0, first_var), scores_chunks(1, 1))
    lax.fori_loop(1, n_blocks // 2, two_steps, 0)
    issue(values_chunks(last - 1, 0), softmax_chunks(1, last_var))
    issue(values_chunks(last, 1))


def _win_attn(q, k4, v4t, rel_bias, bucket_t, sink_rows, seq):
    t = q.shape[0]
    n_blocks = seq // BLOCK
    assert n_blocks >= 2 and n_blocks % 2 == 0
    scores_shape = (N_KV_HEADS, 3 * BLOCK, GROUP * BLOCK)
    kv_tiled = N_KV_HEADS * GROUP_DIM
    seq_rows = lambda width: pl.BlockSpec((seq, width), lambda b: (b, 0))
    return pl.pallas_call(
        _win_attn_kernel,
        grid=(t // seq,),
        in_specs=[pl.BlockSpec(memory_space=pltpu.SMEM),
                  seq_rows(ATTN_Q_DIM), seq_rows(kv_tiled),
                  pl.BlockSpec((n_blocks, kv_tiled, BLOCK), lambda b: (b, 0, 0)),
                  _resident(bucket_t.shape), _resident(sink_rows.shape)],
        out_specs=seq_rows(ATTN_Q_DIM),
        out_shape=jax.ShapeDtypeStruct((t, ATTN_Q_DIM), BF16),
        scratch_shapes=[pltpu.VMEM((3,) + scores_shape, F32)]
                       + [pltpu.VMEM(scores_shape, F32)] * 2
                       + [pltpu.VMEM(scores_shape, BF16)] * 2
                       + [pltpu.VMEM((N_KV_HEADS, 1, GROUP * BLOCK), F32)] * 2,
        compiler_params=_params("arbitrary"),
        name="win_attn",
    )(rel_bias, q, k4, v4t, bucket_t, sink_rows)


def _mem_kv_kernel(m_ref, w_ref, km_ref, vm_ref):
    mb = m_ref[...].astype(BF16)
    km_ref[...] = _dot(mb, w_ref[:, :MEM_DIM]).astype(BF16)
    vm_ref[...] = _dot(mb, w_ref[:, MEM_DIM:]).astype(BF16)


def _mem_kv(mem2, w_mem_kv, tm):
    t = mem2.shape[0]
    out = pl.BlockSpec((tm, MEM_DIM), lambda i: (i, 0))
    return pl.pallas_call(
        _mem_kv_kernel,
        grid=(t // tm,),
        in_specs=[pl.BlockSpec((tm, D_MODEL), lambda i: (i, 0)), _resident(w_mem_kv.shape)],
        out_specs=[out, out],
        out_shape=[jax.ShapeDtypeStruct((t, MEM_DIM), BF16)] * 2,
        compiler_params=_params("parallel"),
        name="mem_kv",
    )(mem2, w_mem_kv)


def _merge_kernel(alpha, x_ref, c_ref, a_ref, qm_ref, km_ref, vm_ref, wg_ref, bg_ref,
                  wc_ref, wa_ref, wm_ref, wo_ref, g_ref, b_ref, h_ref):
    x = x_ref[...]
    xb = x.astype(BF16)

    mem_scale = MEM_HEAD_DIM ** -0.5
    heads = []
    for h in range(N_MEM_HEADS):
        lanes = slice(h * MEM_HEAD_DIM, (h + 1) * MEM_HEAD_DIM)
        s = lax.dot_general(qm_ref[:, lanes], km_ref[:, lanes], (((1,), (1,)), ((), ())),
                            preferred_element_type=F32) * mem_scale
        p = jnp.exp(s - jnp.max(s, axis=-1, keepdims=True))
        denom = jnp.sum(p, axis=-1, keepdims=True)
        heads.append(_dot(p.astype(BF16), vm_ref[:, lanes]) * (1.0 / denom))
    mo = jnp.concatenate(heads, axis=1).astype(BF16)

    def gate(i):
        cols = slice(i * D_MODEL, (i + 1) * D_MODEL)
        return jax.nn.sigmoid(_dot(xb, wg_ref[:, cols]) + bg_ref[:, cols])

    merged = gate(0) * _dot(c_ref[...], wc_ref[...])
    merged = merged + gate(1) * _dot(a_ref[...], wa_ref[...])
    merged = merged + gate(2) * _dot(mo, wm_ref[...])
    out = _dot(merged.astype(BF16), wo_ref[...])
    h_ref[...] = _layer_norm(alpha * x + out, g_ref[...], b_ref[...])


def _merge(alpha, x2, c, a, qm, km, vm, wg, bg, wc, wa, wm, wo, ln_g, ln_b, seq, tm):
    t = x2.shape[0]
    tiles_per_seq = seq // tm
    row = lambda width: pl.BlockSpec((tm, width), lambda i: (i, 0))
    mem_spec = pl.BlockSpec((N_MEM, MEM_DIM), lambda i: (i // tiles_per_seq, 0))
    weights = (wg, bg, wc, wa, wm, wo, ln_g, ln_b)
    return pl.pallas_call(
        functools.partial(_merge_kernel, alpha),
        grid=(t // tm,),
        in_specs=[row(D_MODEL), row(CONV_DIM), row(ATTN_Q_DIM), row(MEM_DIM), mem_spec, mem_spec]
                 + [_resident(w.shape) for w in weights],
        out_specs=row(D_MODEL),
        out_shape=jax.ShapeDtypeStruct((t, D_MODEL), F32),
        compiler_params=_params("parallel"),
        name="merge",
    )(x2, c, a, qm, km, vm, *weights)


def _ffn_kernel(alpha, chunk, h_ref, w1_ref, w2_ref, g_ref, b_ref, o_ref, act_ref):
    h = h_ref[...]
    hb = h.astype(BF16)
    d_ff = w2_ref.shape[0]
    for lo in range(0, d_ff, chunk):
        gate = _dot(hb, w1_ref[:, lo:lo + chunk])
        up = _dot(hb, w1_ref[:, d_ff + lo:d_ff + lo + chunk])
        act_ref[:, lo:lo + chunk] = (gate * jax.nn.sigmoid(gate) * up).astype(BF16)
    ffn = _dot(act_ref[...], w2_ref[...])
    o_ref[...] = _layer_norm(alpha * h + ffn, g_ref[...], b_ref[...])


def _ffn(alpha, h, w1, w2, ln_g, ln_b, tm, chunk):
    t = h.shape[0]
    d_ff = w2.shape[0]
    row = pl.BlockSpec((tm, D_MODEL), lambda i: (i, 0))
    weights = (w1, w2, ln_g, ln_b)
    return pl.pallas_call(
        functools.partial(_ffn_kernel, alpha, chunk),
        grid=(t // tm,),
        in_specs=[row] + [_resident(w.shape) for w in weights],
        out_specs=row,
        out_shape=jax.ShapeDtypeStruct((t, D_MODEL), F32),
        scratch_shapes=[pltpu.VMEM((tm, d_ff), BF16)],
        compiler_params=_params("parallel"),
        name="ffn",
    )(h, *weights)


def _t5_bucket(rel):
    half = N_BUCKETS // 2
    max_exact = half // 2
    base = jnp.where(rel > 0, half, 0)
    n = jnp.abs(rel)
    nf = jnp.maximum(n, 1).astype(jnp.float32)
    large = max_exact + (jnp.log(nf / max_exact) / math.log(MAX_DISTANCE / max_exact)
                         * (half - max_exact)).astype(jnp.int32)
    large = jnp.minimum(large, half - 1)
    return base + jnp.where(n < max_exact, n, large)


def _layer(x2, mem2, bucket, rel_bias, seq, alpha, w_in, b_gate, conv_dw_w, conv_dw_b, conv_ln_g,
           conv_ln_b, w_conv_out, attn_sink, w_attn_out, w_mem_kv, w_mem_out, w_o, ln1_g, ln1_b,
           w_ffn_in, w_ffn_out, ln2_g, ln2_b):
    row = lambda v: v.reshape(1, -1)
    w_in_b = w_in.astype(BF16)
    u, q, k4, v4t, qm = _in_proj(x2, w_in_b[:, :PROJ_DIM], tm=512)
    c = _conv(u, conv_dw_w, row(conv_dw_b), row(conv_ln_g), row(conv_ln_b), seq)
    sink_rows = jnp.repeat(attn_sink.astype(F32).reshape(N_KV_HEADS, GROUP), BLOCK, axis=1)[:, None, :]
    a = _win_attn(q, k4, v4t, rel_bias.astype(F32), bucket, sink_rows, seq)
    km, vm = _mem_kv(mem2, w_mem_kv.astype(BF16), tm=512)
    h = _merge(alpha, x2, c, a, qm, km, vm, w_in_b[:, PROJ_DIM:], row(b_gate),
               w_conv_out.astype(BF16), w_attn_out.astype(BF16), w_mem_out.astype(BF16),
               w_o.astype(BF16), row(ln1_g), row(ln1_b), seq, tm=512)
    return _ffn(alpha, h, w_ffn_in.astype(BF16), w_ffn_out.astype(BF16), row(ln2_g), row(ln2_b),
                tm=512, chunk=256)


def kernel(x, mem, rel_bias, w_in, b_gate, conv_dw_w, conv_dw_b, conv_ln_g, conv_ln_b, w_conv_out,
           attn_sink, w_attn_out, w_mem_kv, w_mem_out, w_o, ln1_g, ln1_b, w_ffn_in, w_ffn_out,
           ln2_g, ln2_b):
    batch, seq, d_model = x.shape
    depth = w_in.shape[0]
    assert d_model == D_MODEL and mem.shape[1:] == (N_MEM, D_MODEL) and seq % BLOCK == 0
    alpha = (2 * depth) ** 0.25
    qloc = jnp.arange(BLOCK, dtype=jnp.int32)
    kloc = jnp.arange(3 * BLOCK, dtype=jnp.int32) - BLOCK
    bucket = _t5_bucket(kloc[:, None] - qloc[None, :])
    x2 = x.reshape(batch * seq, d_model)
    mem2 = mem.reshape(batch * N_MEM, d_model)
    per_layer = (w_in, b_gate, conv_dw_w, conv_dw_b, conv_ln_g, conv_ln_b, w_conv_out, attn_sink,
                 w_attn_out, w_mem_kv, w_mem_out, w_o, ln1_g, ln1_b, w_ffn_in, w_ffn_out, ln2_g, ln2_b)
    for l in range(depth):
        x2 = _layer(x2, mem2, bucket, rel_bias, seq, alpha, *(p[l] for p in per_layer))
    return x2.reshape(batch, seq, d_model)
```

```python
import functools
import math

import jax
import jax.numpy as jnp
from jax import lax
from jax.experimental import pallas as pl
from jax.experimental.pallas import tpu as pltpu

D_MODEL = 1024
N_MEM = 256
CONV_DIM = 512
CONV_WIDTH = 31
N_HEADS = 8
N_KV_HEADS = 2
HEAD_DIM = 64
WINDOW = 128
BLOCK = 128
N_MEM_HEADS = 4
MEM_HEAD_DIM = 128
N_BUCKETS = 32
MAX_DISTANCE = 128
N_BRANCHES = 3
LN_EPS = 1e-5
NEG_INF = -1e30

GROUP = N_HEADS // N_KV_HEADS
ATTN_Q_DIM = N_HEADS * HEAD_DIM
KV_DIM = N_KV_HEADS * HEAD_DIM
MEM_DIM = N_MEM_HEADS * MEM_HEAD_DIM
GROUP_DIM = GROUP * HEAD_DIM
ATTN_SCALE = HEAD_DIM ** -0.5
LOG2E = math.log2(math.e)
PROJ_DIM = 2 * CONV_DIM + ATTN_Q_DIM + 2 * KV_DIM + MEM_DIM

V7X_LANES = 128
V7X_SUBLANES = 8
V7X_VMEM_LIMIT_BYTES = 56 * 1024 * 1024

BF16 = jnp.bfloat16
F32 = jnp.float32


def _params(*semantics):
    return pltpu.CompilerParams(dimension_semantics=semantics,
                                vmem_limit_bytes=V7X_VMEM_LIMIT_BYTES)


def _resident(shape):
    nd = len(shape)
    return pl.BlockSpec(shape, lambda *_: (0,) * nd, pipeline_mode=pl.Buffered(1))


def _dot(a, b):
    return jnp.dot(a, b, preferred_element_type=F32)


def _layer_norm(v, g, b):
    mu = jnp.mean(v, axis=-1, keepdims=True)
    d = v - mu
    var = jnp.mean(d * d, axis=-1, keepdims=True)
    return d * lax.rsqrt(var + LN_EPS) * g + b


def _in_proj_kernel(x_ref, w_ref, u_ref, q_ref, k4_ref, v4t_ref, qm_ref):
    xb = x_ref[...].astype(BF16)

    def proj(lo, width):
        return _dot(xb, w_ref[:, lo:lo + width])

    a = proj(0, CONV_DIM)
    g = proj(CONV_DIM, CONV_DIM)
    u_ref[...] = a * jax.nn.sigmoid(g)
    off = 2 * CONV_DIM
    q_ref[...] = (proj(off, ATTN_Q_DIM) * (ATTN_SCALE * LOG2E)).astype(BF16)
    off += ATTN_Q_DIM
    k = proj(off, KV_DIM)
    v = proj(off + KV_DIM, KV_DIM)
    off += 2 * KV_DIM
    qm_ref[...] = proj(off, MEM_DIM).astype(BF16)

    first_half = lax.broadcasted_iota(jnp.int32, k.shape, 1) < HEAD_DIM
    swapped = pltpu.roll(k, HEAD_DIM, axis=1)
    h0 = jnp.where(first_half, k, swapped).astype(BF16)
    h1 = jnp.where(first_half, swapped, k).astype(BF16)
    k4_ref[...] = jnp.concatenate([h0, h0, h1, h1], axis=1)

    vt = v.T.astype(BF16)
    for c in range(v4t_ref.shape[0]):
        keys = vt[:, c * BLOCK:(c + 1) * BLOCK]
        v4t_ref[c] = jnp.concatenate([keys[:HEAD_DIM]] * GROUP + [keys[HEAD_DIM:]] * GROUP, axis=0)


def _in_proj(x2, w_proj, tm):
    t = x2.shape[0]
    row = lambda width: pl.BlockSpec((tm, width), lambda i: (i, 0))
    kv_tiled = N_KV_HEADS * GROUP_DIM
    out_specs = [row(CONV_DIM), row(ATTN_Q_DIM), row(kv_tiled),
                 pl.BlockSpec((tm // BLOCK, kv_tiled, BLOCK), lambda i: (i, 0, 0)), row(MEM_DIM)]
    out_shape = [jax.ShapeDtypeStruct((t, CONV_DIM), F32),
                 jax.ShapeDtypeStruct((t, ATTN_Q_DIM), BF16),
                 jax.ShapeDtypeStruct((t, kv_tiled), BF16),
                 jax.ShapeDtypeStruct((t // BLOCK, kv_tiled, BLOCK), BF16),
                 jax.ShapeDtypeStruct((t, MEM_DIM), BF16)]
    return pl.pallas_call(
        _in_proj_kernel,
        grid=(t // tm,),
        in_specs=[row(D_MODEL), _resident(w_proj.shape)],
        out_specs=out_specs,
        out_shape=out_shape,
        compiler_params=_params("parallel"),
        name="in_proj",
    )(x2, w_proj)


CONV_HALO = CONV_WIDTH // 2
CONV_EDGE_ROWS = 16
_CONV_GROUPS = 8


def _tile_conv(u_ref, prev_ref, next_ref, w_ref, b_ref, g_ref, beta_ref, is_first, is_last,
               seq_ref, taps_ref):
    sub, n, _ = u_ref.shape
    for k in range(CONV_WIDTH):
        taps_ref[k] = jnp.broadcast_to(w_ref[k:k + 1, :], (sub, CONV_DIM))
    for a0 in range(0, n, sub):
        seq_ref[CONV_HALO + a0:CONV_HALO + a0 + sub] = pltpu.einshape("rac->arc", u_ref[:, a0:a0 + sub, :])

    phase = lax.broadcasted_iota(jnp.int32, (CONV_HALO, sub, CONV_DIM), 1)
    outside_before = jnp.where(is_first, 0.0, prev_ref[CONV_EDGE_ROWS - CONV_HALO:, :])
    before = pltpu.roll(seq_ref[n:n + CONV_HALO], 1, axis=1)
    seq_ref[0:CONV_HALO] = jnp.where(phase == 0, outside_before[:, None, :], before)
    outside_after = jnp.where(is_last, 0.0, next_ref[:CONV_HALO, :])
    after = pltpu.roll(seq_ref[CONV_HALO:2 * CONV_HALO], sub - 1, axis=1)
    seq_ref[CONV_HALO + n:2 * CONV_HALO + n] = jnp.where(phase == sub - 1, outside_after[:, None, :], after)

    done = []
    for a0 in range(0, n, _CONV_GROUPS):
        acc = jnp.zeros((_CONV_GROUPS, sub, CONV_DIM), F32)
        for k in range(CONV_WIDTH):
            acc = acc + seq_ref[a0 + k:a0 + k + _CONV_GROUPS] * taps_ref[k]
        y = _layer_norm(acc + b_ref[...], g_ref[...], beta_ref[...])
        done.append(y * jax.nn.sigmoid(y))
    y = pltpu.einshape("arc->rac", jnp.concatenate(done, axis=0))
    return y.reshape(sub * n, CONV_DIM)


def _win_attn_kernel(rb_ref, q_ref, k4_ref, v4t_ref, bucket_ref, sink_ref, o_ref,
                     bias_ref, s0_ref, s1_ref, p0_ref, p1_ref, inv0_ref, inv1_ref):
    s_ref, p_ref, inv_ref = (s0_ref, s1_ref), (p0_ref, p1_ref), (inv0_ref, inv1_ref)
    n_blocks = v4t_ref.shape[0]
    last = n_blocks - 1
    keys = 3 * BLOCK
    first_var, mid_var, last_var = 0, 1, 2

    @pl.when(pl.program_id(0) == 0)
    def _():
        bucket = bucket_ref[...]
        kpos = lax.broadcasted_iota(jnp.int32, (keys, BLOCK), 0) - BLOCK
        qpos = lax.broadcasted_iota(jnp.int32, (keys, BLOCK), 1)
        in_band = jnp.abs(kpos - qpos) <= WINDOW
        for h in range(N_HEADS):
            acc = jnp.zeros((keys, BLOCK), F32)
            for bkt in range(N_BUCKETS):
                acc = jnp.where(bucket == bkt, rb_ref[bkt, h] * LOG2E, acc)
            acc = jnp.where(in_band, acc, NEG_INF)
            j, g = divmod(h, GROUP)
            cols = slice(g * BLOCK, (g + 1) * BLOCK)
            bias_ref[mid_var, j, :, cols] = acc
            bias_ref[first_var, j, :, cols] = jnp.where(kpos >= 0, acc, NEG_INF)
            bias_ref[last_var, j, :, cols] = jnp.where(kpos < BLOCK, acc, NEG_INF)

    head_of_lane = lax.broadcasted_iota(jnp.int32, (BLOCK, GROUP_DIM), 1) // HEAD_DIM

    def window(n):
        return [jnp.maximum(n - 1, 0), n, jnp.minimum(n + 1, last)]

    def rows_of(n):
        return pl.ds(pl.multiple_of(n * BLOCK, BLOCK), BLOCK)

    halves = [(j, h) for j in range(N_KV_HEADS) for h in range(2)]
    half_cols = GROUP * BLOCK // 2

    def scores_chunks(n, slot):
        def chunk(j, h):
            lanes = slice(j * GROUP_DIM, (j + 1) * GROUP_DIM)
            qg = q_ref[rows_of(n), lanes]
            qm = jnp.concatenate([jnp.where(head_of_lane == g, qg, jnp.zeros_like(qg))
                                  for g in (2 * h, 2 * h + 1)], axis=0)
            kwin = jnp.concatenate([k4_ref[rows_of(b), lanes] for b in window(n)], axis=0)
            s_ref[slot][j, :, h * half_cols:(h + 1) * half_cols] = lax.dot_general(
                kwin, qm, (((1,), (1,)), ((), ())), preferred_element_type=F32)
        return [functools.partial(chunk, j, h) for j, h in halves]

    def softmax_chunks(slot, var):
        def chunk(j, h):
            for g in (2 * h, 2 * h + 1):
                cols = slice(g * BLOCK, (g + 1) * BLOCK)
                s = s_ref[slot][j, :, cols] + bias_ref[var, j, :, cols]
                sink = sink_ref[j, :, cols] * LOG2E
                m = jnp.maximum(jnp.max(s, axis=0, keepdims=True), sink)
                p = jnp.exp2(s - m)
                denom = jnp.sum(p, axis=0, keepdims=True) + jnp.exp2(sink - m)
                p_ref[slot][j, :, cols] = p.astype(BF16)
                inv_ref[slot][j, :, cols] = 1.0 / denom
        return [functools.partial(chunk, j, h) for j, h in halves]

    def values_chunks(n, slot):
        heads = []

        def chunk(j, h):
            lanes = slice(j * GROUP_DIM, (j + 1) * GROUP_DIM)
            vwin = jnp.concatenate([v4t_ref[b, lanes, :] for b in window(n)], axis=1)
            pv = _dot(vwin, p_ref[slot][j, :, h * half_cols:(h + 1) * half_cols])
            for i, g in enumerate((2 * h, 2 * h + 1)):
                inv = inv_ref[slot][j, :, g * BLOCK:(g + 1) * BLOCK]
                heads.append(pv[g * HEAD_DIM:(g + 1) * HEAD_DIM, i * BLOCK:(i + 1) * BLOCK] * inv)
            if len(heads) == N_HEADS:
                o_ref[rows_of(n), :] = jnp.concatenate(heads, axis=0).T.astype(BF16)
        return [functools.partial(chunk, j, h) for j, h in halves]

    def issue(*stages):
        for chunks in zip(*stages):
            for chunk in chunks:
                chunk()

    def two_steps(i, carry):
        for slot in range(2):
            t = 2 * i + slot
            issue(values_chunks(t - 2, slot), softmax_chunks(1 - slot, mid_var), scores_chunks(t, slot))
        return carry

    issue(scores_chunks(0, 0))
    issue(softmax_chunks(0, first_var), scores_chunks(1, 1))
    lax.fori_loop(1, n_blocks // 2, two_steps, 0)
    issue(values_chunks(last - 1, 0), softmax_chunks(1, last_var))
    issue(values_chunks(last, 1))


def _win_attn(q, k4, v4t, rel_bias, bucket_t, sink_rows, seq):
    t = q.shape[0]
    n_blocks = seq // BLOCK
    assert n_blocks >= 2 and n_blocks % 2 == 0
    scores_shape = (N_KV_HEADS, 3 * BLOCK, GROUP * BLOCK)
    kv_tiled = N_KV_HEADS * GROUP_DIM
    seq_rows = lambda width: pl.BlockSpec((seq, width), lambda b: (b, 0))
    return pl.pallas_call(
        _win_attn_kernel,
        grid=(t // seq,),
        in_specs=[pl.BlockSpec(memory_space=pltpu.SMEM),
                  seq_rows(ATTN_Q_DIM), seq_rows(kv_tiled),
                  pl.BlockSpec((n_blocks, kv_tiled, BLOCK), lambda b: (b, 0, 0)),
                  _resident(bucket_t.shape), _resident(sink_rows.shape)],
        out_specs=seq_rows(ATTN_Q_DIM),
        out_shape=jax.ShapeDtypeStruct((t, ATTN_Q_DIM), BF16),
        scratch_shapes=[pltpu.VMEM((3,) + scores_shape, F32)]
                       + [pltpu.VMEM(scores_shape, F32)] * 2
                       + [pltpu.VMEM(scores_shape, BF16)] * 2
                       + [pltpu.VMEM((N_KV_HEADS, 1, GROUP * BLOCK), F32)] * 2,
        compiler_params=_params("arbitrary"),
        name="win_attn",
    )(rel_bias, q, k4, v4t, bucket_t, sink_rows)


def _mem_kv_kernel(m_ref, w_ref, km_ref, vm_ref):
    mb = m_ref[...].astype(BF16)
    km_ref[...] = _dot(mb, w_ref[:, :MEM_DIM]).astype(BF16)
    vm_ref[...] = _dot(mb, w_ref[:, MEM_DIM:]).astype(BF16)


def _mem_kv(mem2, w_mem_kv, tm):
    t = mem2.shape[0]
    out = pl.BlockSpec((tm, MEM_DIM), lambda i: (i, 0))
    return pl.pallas_call(
        _mem_kv_kernel,
        grid=(t // tm,),
        in_specs=[pl.BlockSpec((tm, D_MODEL), lambda i: (i, 0)), _resident(w_mem_kv.shape)],
        out_specs=[out, out],
        out_shape=[jax.ShapeDtypeStruct((t, MEM_DIM), BF16)] * 2,
        compiler_params=_params("parallel"),
        name="mem_kv",
    )(mem2, w_mem_kv)


def _merge_kernel(alpha, tiles_per_seq, x_ref, u_ref, u_prev_ref, u_next_ref, a_ref, qm_ref,
                  km_ref, vm_ref, cw_ref, cb_ref, cg_ref, cbeta_ref, wg_ref, bg_ref,
                  wc_ref, wa_ref, wm_ref, wo_ref, g_ref, b_ref, h_ref, seq_ref, taps_ref):
    x = x_ref[...]
    xb = x.astype(BF16)
    tile_in_seq = pl.program_id(0) % tiles_per_seq
    c = _tile_conv(u_ref, u_prev_ref, u_next_ref, cw_ref, cb_ref, cg_ref, cbeta_ref,
                   tile_in_seq == 0, tile_in_seq == tiles_per_seq - 1, seq_ref, taps_ref)

    mem_scale = MEM_HEAD_DIM ** -0.5
    heads = []
    for h in range(N_MEM_HEADS):
        lanes = slice(h * MEM_HEAD_DIM, (h + 1) * MEM_HEAD_DIM)
        s = lax.dot_general(qm_ref[:, lanes], km_ref[:, lanes], (((1,), (1,)), ((), ())),
                            preferred_element_type=F32) * mem_scale
        p = jnp.exp(s - jnp.max(s, axis=-1, keepdims=True))
        denom = jnp.sum(p, axis=-1, keepdims=True)
        heads.append(_dot(p.astype(BF16), vm_ref[:, lanes]) * (1.0 / denom))
    mo = jnp.concatenate(heads, axis=1).astype(BF16)

    def gate(i):
        cols = slice(i * D_MODEL, (i + 1) * D_MODEL)
        return jax.nn.sigmoid(_dot(xb, wg_ref[:, cols]) + bg_ref[:, cols])

    merged = gate(0) * _dot(c.astype(BF16), wc_ref[...])
    merged = merged + gate(1) * _dot(a_ref[...], wa_ref[...])
    merged = merged + gate(2) * _dot(mo, wm_ref[...])
    out = _dot(merged.astype(BF16), wo_ref[...])
    h_ref[...] = _layer_norm(alpha * x + out, g_ref[...], b_ref[...])


def _merge(alpha, x2, u, a, qm, km, vm, conv_w, conv_b, conv_g, conv_beta, wg, bg, wc, wa, wm, wo,
           ln_g, ln_b, seq, tm):
    t = x2.shape[0]
    tiles_per_seq = seq // tm
    groups = tm // V7X_SUBLANES
    edge_blocks = tm // CONV_EDGE_ROWS
    assert groups >= CONV_HALO and groups % _CONV_GROUPS == 0 and CONV_EDGE_ROWS >= CONV_HALO
    row = lambda width: pl.BlockSpec((tm, width), lambda i: (i, 0))
    edge = lambda index_map: pl.BlockSpec((CONV_EDGE_ROWS, CONV_DIM), index_map)
    mem_spec = pl.BlockSpec((N_MEM, MEM_DIM), lambda i: (i // tiles_per_seq, 0))
    weights = (conv_w, conv_b, conv_g, conv_beta, wg, bg, wc, wa, wm, wo, ln_g, ln_b)
    return pl.pallas_call(
        functools.partial(_merge_kernel, alpha, tiles_per_seq),
        grid=(t // tm,),
        in_specs=[row(D_MODEL),
                  pl.BlockSpec((V7X_SUBLANES, groups, CONV_DIM), lambda i: (i, 0, 0)),
                  edge(lambda i: (jnp.maximum(i * edge_blocks - 1, 0), 0)),
                  edge(lambda i: (jnp.minimum((i + 1) * edge_blocks, t // CONV_EDGE_ROWS - 1), 0)),
                  row(ATTN_Q_DIM), row(MEM_DIM), mem_spec, mem_spec]
                 + [_resident(w.shape) for w in weights],
        out_specs=row(D_MODEL),
        out_shape=jax.ShapeDtypeStruct((t, D_MODEL), F32),
        scratch_shapes=[pltpu.VMEM((groups + 2 * CONV_HALO, V7X_SUBLANES, CONV_DIM), F32),
                        pltpu.VMEM((CONV_WIDTH, V7X_SUBLANES, CONV_DIM), F32)],
        compiler_params=_params("parallel"),
        name="merge",
    )(x2, u.reshape(t // groups, groups, CONV_DIM), u, u, a, qm, km, vm, *weights)


def _ffn_kernel(alpha, chunk, h_ref, w1_ref, w2_ref, g_ref, b_ref, o_ref, act_ref):
    h = h_ref[...]
    hb = h.astype(BF16)
    d_ff = w2_ref.shape[0]
    for lo in range(0, d_ff, chunk):
        gate = _dot(hb, w1_ref[:, lo:lo + chunk])
        up = _dot(hb, w1_ref[:, d_ff + lo:d_ff + lo + chunk])
        act_ref[:, lo:lo + chunk] = (gate * jax.nn.sigmoid(gate) * up).astype(BF16)
    ffn = _dot(act_ref[...], w2_ref[...])
    o_ref[...] = _layer_norm(alpha * h + ffn, g_ref[...], b_ref[...])


def _ffn(alpha, h, w1, w2, ln_g, ln_b, tm, chunk):
    t = h.shape[0]
    d_ff = w2.shape[0]
    row = pl.BlockSpec((tm, D_MODEL), lambda i: (i, 0))
    weights = (w1, w2, ln_g, ln_b)
    return pl.pallas_call(
        functools.partial(_ffn_kernel, alpha, chunk),
        grid=(t // tm,),
        in_specs=[row] + [_resident(w.shape) for w in weights],
        out_specs=row,
        out_shape=jax.ShapeDtypeStruct((t, D_MODEL), F32),
        scratch_shapes=[pltpu.VMEM((tm, d_ff), BF16)],
        compiler_params=_params("parallel"),
        name="ffn",
    )(h, *weights)


def _t5_bucket(rel):
    half = N_BUCKETS // 2
    max_exact = half // 2
    base = jnp.where(rel > 0, half, 0)
    n = jnp.abs(rel)
    nf = jnp.maximum(n, 1).astype(jnp.float32)
    large = max_exact + (jnp.log(nf / max_exact) / math.log(MAX_DISTANCE / max_exact)
                         * (half - max_exact)).astype(jnp.int32)
    large = jnp.minimum(large, half - 1)
    return base + jnp.where(n < max_exact, n, large)


def _layer(x2, mem2, bucket, rel_bias, seq, alpha, w_in, b_gate, conv_dw_w, conv_dw_b, conv_ln_g,
           conv_ln_b, w_conv_out, attn_sink, w_attn_out, w_mem_kv, w_mem_out, w_o, ln1_g, ln1_b,
           w_ffn_in, w_ffn_out, ln2_g, ln2_b):
    row = lambda v: v.reshape(1, -1)
    w_proj = w_in[:, :PROJ_DIM].astype(BF16)
    w_gate = w_in[:, PROJ_DIM:].astype(BF16)
    u, q, k4, v4t, qm = _in_proj(x2, w_proj, tm=512)
    sink_rows = jnp.repeat(attn_sink.astype(F32).reshape(N_KV_HEADS, GROUP), BLOCK, axis=1)[:, None, :]
    a = _win_attn(q, k4, v4t, rel_bias.astype(F32), bucket, sink_rows, seq)
    km, vm = _mem_kv(mem2, w_mem_kv.astype(BF16), tm=512)
    h = _merge(alpha, x2, u, a, qm, km, vm, conv_dw_w, row(conv_dw_b), row(conv_ln_g), row(conv_ln_b),
               w_gate, row(b_gate),
               w_conv_out.astype(BF16), w_attn_out.astype(BF16), w_mem_out.astype(BF16),
               w_o.astype(BF16), row(ln1_g), row(ln1_b), seq, tm=512)
    return _ffn(alpha, h, w_ffn_in.astype(BF16), w_ffn_out.astype(BF16), row(ln2_g), row(ln2_b),
                tm=1024, chunk=256)


def kernel(x, mem, rel_bias, w_in, b_gate, conv_dw_w, conv_dw_b, conv_ln_g, conv_ln_b, w_conv_out,
           attn_sink, w_attn_out, w_mem_kv, w_mem_out, w_o, ln1_g, ln1_b, w_ffn_in, w_ffn_out,
           ln2_g, ln2_b):
    batch, seq, d_model = x.shape
    depth = w_in.shape[0]
    assert d_model == D_MODEL and mem.shape[1:] == (N_MEM, D_MODEL) and seq % BLOCK == 0
    alpha = (2 * depth) ** 0.25
    qloc = jnp.arange(BLOCK, dtype=jnp.int32)
    kloc = jnp.arange(3 * BLOCK, dtype=jnp.int32) - BLOCK
    bucket = _t5_bucket(kloc[:, None] - qloc[None, :])
    x2 = x.reshape(batch * seq, d_model)
    mem2 = mem.reshape(batch * N_MEM, d_model)
    per_layer = (w_in, b_gate, conv_dw_w, conv_dw_b, conv_ln_g, conv_ln_b, w_conv_out, attn_sink,
                 w_attn_out, w_mem_kv, w_mem_out, w_o, ln1_g, ln1_b, w_ffn_in, w_ffn_out, ln2_g, ln2_b)
    for l in range(depth):
        x2 = _layer(x2, mem2, bucket, rel_bias, seq, alpha, *(p[l] for p in per_layer))
    return x2.reshape(batch, seq, d_model)
```

```python
import functools
import math

import jax
import jax.numpy as jnp
from jax import lax
from jax.experimental import pallas as pl
from jax.experimental.pallas import tpu as pltpu

D_MODEL = 1024
N_MEM = 256
CONV_DIM = 512
CONV_WIDTH = 31
N_HEADS = 8
N_KV_HEADS = 2
HEAD_DIM = 64
WINDOW = 128
BLOCK = 128
N_MEM_HEADS = 4
MEM_HEAD_DIM = 128
N_BUCKETS = 32
MAX_DISTANCE = 128
N_BRANCHES = 3
LN_EPS = 1e-5
NEG_INF = -1e30

GROUP = N_HEADS // N_KV_HEADS
ATTN_Q_DIM = N_HEADS * HEAD_DIM
KV_DIM = N_KV_HEADS * HEAD_DIM
MEM_DIM = N_MEM_HEADS * MEM_HEAD_DIM
GROUP_DIM = GROUP * HEAD_DIM
ATTN_SCALE = HEAD_DIM ** -0.5
LOG2E = math.log2(math.e)
PROJ_DIM = 2 * CONV_DIM + ATTN_Q_DIM + 2 * KV_DIM + MEM_DIM

V7X_LANES = 128
V7X_SUBLANES = 8
V7X_VMEM_LIMIT_BYTES = 56 * 1024 * 1024

BF16 = jnp.bfloat16
F32 = jnp.float32


def _params(*semantics):
    return pltpu.CompilerParams(dimension_semantics=semantics,
                                vmem_limit_bytes=V7X_VMEM_LIMIT_BYTES)


def _resident(shape):
    nd = len(shape)
    return pl.BlockSpec(shape, lambda *_: (0,) * nd, pipeline_mode=pl.Buffered(1))


def _dot(a, b):
    return jnp.dot(a, b, preferred_element_type=F32)


_EPILOGUE_ROWS = 256


def _layer_norm(v, g, b):
    mu = jnp.mean(v, axis=-1, keepdims=True)
    d = v - mu
    var = jnp.mean(d * d, axis=-1, keepdims=True)
    return d * lax.rsqrt(var + LN_EPS) * g + b


def _in_proj_kernel(x_ref, w_ref, u_ref, q_ref, k4_ref, v4t_ref, qm_ref):
    xb = x_ref[...].astype(BF16)

    def proj(lo, width):
        return _dot(xb, w_ref[:, lo:lo + width])

    a = proj(0, CONV_DIM)
    g = proj(CONV_DIM, CONV_DIM)
    u_ref[...] = a * jax.nn.sigmoid(g)
    off = 2 * CONV_DIM
    q_ref[...] = (proj(off, ATTN_Q_DIM) * (ATTN_SCALE * LOG2E)).astype(BF16)
    off += ATTN_Q_DIM
    k = proj(off, KV_DIM)
    v = proj(off + KV_DIM, KV_DIM)
    off += 2 * KV_DIM
    qm_ref[...] = proj(off, MEM_DIM).astype(BF16)

    first_half = lax.broadcasted_iota(jnp.int32, k.shape, 1) < HEAD_DIM
    swapped = pltpu.roll(k, HEAD_DIM, axis=1)
    h0 = jnp.where(first_half, k, swapped).astype(BF16)
    h1 = jnp.where(first_half, swapped, k).astype(BF16)
    k4_ref[...] = jnp.concatenate([h0, h0, h1, h1], axis=1)

    vt = v.T.astype(BF16)
    for c in range(v4t_ref.shape[0]):
        keys = vt[:, c * BLOCK:(c + 1) * BLOCK]
        v4t_ref[c] = jnp.concatenate([keys[:HEAD_DIM]] * GROUP + [keys[HEAD_DIM:]] * GROUP, axis=0)


def _in_proj(x2, w_in, tm):
    t = x2.shape[0]
    row = lambda width: pl.BlockSpec((tm, width), lambda i: (i, 0))
    kv_tiled = N_KV_HEADS * GROUP_DIM
    out_specs = [row(CONV_DIM), row(ATTN_Q_DIM), row(kv_tiled),
                 pl.BlockSpec((tm // BLOCK, kv_tiled, BLOCK), lambda i: (i, 0, 0)), row(MEM_DIM)]
    out_shape = [jax.ShapeDtypeStruct((t, CONV_DIM), F32),
                 jax.ShapeDtypeStruct((t, ATTN_Q_DIM), BF16),
                 jax.ShapeDtypeStruct((t, kv_tiled), BF16),
                 jax.ShapeDtypeStruct((t // BLOCK, kv_tiled, BLOCK), BF16),
                 jax.ShapeDtypeStruct((t, MEM_DIM), BF16)]
    return pl.pallas_call(
        _in_proj_kernel,
        grid=(t // tm,),
        in_specs=[row(D_MODEL),
                  pl.BlockSpec((D_MODEL, PROJ_DIM), lambda i: (0, 0), pipeline_mode=pl.Buffered(1))],
        out_specs=out_specs,
        out_shape=out_shape,
        compiler_params=_params("parallel"),
        name="in_proj",
    )(x2, w_in)


CONV_HALO = CONV_WIDTH // 2
CONV_EDGE_ROWS = 16
_CONV_GROUPS = 8


def _tile_conv(u_ref, prev_ref, next_ref, w_ref, b_ref, g_ref, beta_ref, is_first, is_last,
               seq_ref, taps_ref):
    sub, n, _ = u_ref.shape
    for k in range(CONV_WIDTH):
        taps_ref[k] = jnp.broadcast_to(w_ref[k:k + 1, :], (sub, CONV_DIM))
    for a0 in range(0, n, sub):
        seq_ref[CONV_HALO + a0:CONV_HALO + a0 + sub] = pltpu.einshape("rac->arc", u_ref[:, a0:a0 + sub, :])

    phase = lax.broadcasted_iota(jnp.int32, (CONV_HALO, sub, CONV_DIM), 1)
    outside_before = jnp.where(is_first, 0.0, prev_ref[CONV_EDGE_ROWS - CONV_HALO:, :])
    before = pltpu.roll(seq_ref[n:n + CONV_HALO], 1, axis=1)
    seq_ref[0:CONV_HALO] = jnp.where(phase == 0, outside_before[:, None, :], before)
    outside_after = jnp.where(is_last, 0.0, next_ref[:CONV_HALO, :])
    after = pltpu.roll(seq_ref[CONV_HALO:2 * CONV_HALO], sub - 1, axis=1)
    seq_ref[CONV_HALO + n:2 * CONV_HALO + n] = jnp.where(phase == sub - 1, outside_after[:, None, :], after)

    done = []
    for a0 in range(0, n, _CONV_GROUPS):
        acc = jnp.zeros((_CONV_GROUPS, sub, CONV_DIM), F32)
        for k in range(CONV_WIDTH):
            acc = acc + seq_ref[a0 + k:a0 + k + _CONV_GROUPS] * taps_ref[k]
        y = _layer_norm(acc + b_ref[...], g_ref[...], beta_ref[...])
        done.append(y * jax.nn.sigmoid(y))
    y = pltpu.einshape("arc->rac", jnp.concatenate(done, axis=0))
    return y.reshape(sub * n, CONV_DIM)


def _win_attn_kernel(rb_ref, q_ref, k4_ref, v4t_ref, bucket_ref, sink_ref, o_ref,
                     bias_ref, s0_ref, s1_ref, p0_ref, p1_ref, inv0_ref, inv1_ref):
    s_ref, p_ref, inv_ref = (s0_ref, s1_ref), (p0_ref, p1_ref), (inv0_ref, inv1_ref)
    n_blocks = v4t_ref.shape[0]
    last = n_blocks - 1
    keys = 3 * BLOCK
    first_var, mid_var, last_var = 0, 1, 2

    @pl.when(pl.program_id(0) == 0)
    def _():
        bucket = bucket_ref[...]
        kpos = lax.broadcasted_iota(jnp.int32, (keys, BLOCK), 0) - BLOCK
        qpos = lax.broadcasted_iota(jnp.int32, (keys, BLOCK), 1)
        in_band = jnp.abs(kpos - qpos) <= WINDOW
        for h in range(N_HEADS):
            acc = jnp.zeros((keys, BLOCK), F32)
            for bkt in range(N_BUCKETS):
                acc = jnp.where(bucket == bkt, rb_ref[bkt, h] * LOG2E, acc)
            acc = jnp.where(in_band, acc, NEG_INF)
            j, g = divmod(h, GROUP)
            cols = slice(g * BLOCK, (g + 1) * BLOCK)
            bias_ref[mid_var, j, :, cols] = acc
            bias_ref[first_var, j, :, cols] = jnp.where(kpos >= 0, acc, NEG_INF)
            bias_ref[last_var, j, :, cols] = jnp.where(kpos < BLOCK, acc, NEG_INF)

    head_of_lane = lax.broadcasted_iota(jnp.int32, (BLOCK, GROUP_DIM), 1) // HEAD_DIM

    def window(n):
        return [jnp.maximum(n - 1, 0), n, jnp.minimum(n + 1, last)]

    def rows_of(n):
        return pl.ds(pl.multiple_of(n * BLOCK, BLOCK), BLOCK)

    halves = [(j, h) for j in range(N_KV_HEADS) for h in range(2)]
    half_cols = GROUP * BLOCK // 2

    def scores_chunks(n, slot):
        def chunk(j, h):
            lanes = slice(j * GROUP_DIM, (j + 1) * GROUP_DIM)
            qg = q_ref[rows_of(n), lanes]
            qm = jnp.concatenate([jnp.where(head_of_lane == g, qg, jnp.zeros_like(qg))
                                  for g in (2 * h, 2 * h + 1)], axis=0)
            kwin = jnp.concatenate([k4_ref[rows_of(b), lanes] for b in window(n)], axis=0)
            s_ref[slot][j, :, h * half_cols:(h + 1) * half_cols] = lax.dot_general(
                kwin, qm, (((1,), (1,)), ((), ())), preferred_element_type=F32)
        return [functools.partial(chunk, j, h) for j, h in halves]

    def softmax_chunks(slot, var):
        def chunk(j, h):
            for g in (2 * h, 2 * h + 1):
                cols = slice(g * BLOCK, (g + 1) * BLOCK)
                s = s_ref[slot][j, :, cols] + bias_ref[var, j, :, cols]
                sink = sink_ref[j, :, cols] * LOG2E
                m = jnp.maximum(jnp.max(s, axis=0, keepdims=True), sink)
                p = jnp.exp2(s - m)
                denom = jnp.sum(p, axis=0, keepdims=True) + jnp.exp2(sink - m)
                p_ref[slot][j, :, cols] = p.astype(BF16)
                inv_ref[slot][j, :, cols] = 1.0 / denom
        return [functools.partial(chunk, j, h) for j, h in halves]

    def values_chunks(n, slot):
        heads = []

        def chunk(j, h):
            lanes = slice(j * GROUP_DIM, (j + 1) * GROUP_DIM)
            vwin = jnp.concatenate([v4t_ref[b, lanes, :] for b in window(n)], axis=1)
            pv = _dot(vwin, p_ref[slot][j, :, h * half_cols:(h + 1) * half_cols])
            for i, g in enumerate((2 * h, 2 * h + 1)):
                inv = inv_ref[slot][j, :, g * BLOCK:(g + 1) * BLOCK]
                heads.append(pv[g * HEAD_DIM:(g + 1) * HEAD_DIM, i * BLOCK:(i + 1) * BLOCK] * inv)
            if len(heads) == N_HEADS:
                o_ref[rows_of(n), :] = jnp.concatenate(heads, axis=0).T.astype(BF16)
        return [functools.partial(chunk, j, h) for j, h in halves]

    def issue(*stages):
        for chunks in zip(*stages):
            for chunk in chunks:
                chunk()

    def two_steps(i, carry):
        for slot in range(2):
            t = 2 * i + slot
            issue(values_chunks(t - 2, slot), softmax_chunks(1 - slot, mid_var), scores_chunks(t, slot))
        return carry

    issue(scores_chunks(0, 0))
    issue(softmax_chunks(0, first_var), scores_chunks(1, 1))
    lax.fori_loop(1, n_blocks // 2, two_steps, 0)
    issue(values_chunks(last - 1, 0), softmax_chunks(1, last_var))
    issue(values_chunks(last, 1))


def _win_attn(q, k4, v4t, rel_bias, bucket_t, sink_rows, seq):
    t = q.shape[0]
    n_blocks = seq // BLOCK
    assert n_blocks >= 2 and n_blocks % 2 == 0
    scores_shape = (N_KV_HEADS, 3 * BLOCK, GROUP * BLOCK)
    kv_tiled = N_KV_HEADS * GROUP_DIM
    seq_rows = lambda width: pl.BlockSpec((seq, width), lambda b: (b, 0))
    return pl.pallas_call(
        _win_attn_kernel,
        grid=(t // seq,),
        in_specs=[pl.BlockSpec(memory_space=pltpu.SMEM),
                  seq_rows(ATTN_Q_DIM), seq_rows(kv_tiled),
                  pl.BlockSpec((n_blocks, kv_tiled, BLOCK), lambda b: (b, 0, 0)),
                  _resident(bucket_t.shape), _resident(sink_rows.shape)],
        out_specs=seq_rows(ATTN_Q_DIM),
        out_shape=jax.ShapeDtypeStruct((t, ATTN_Q_DIM), BF16),
        scratch_shapes=[pltpu.VMEM((3,) + scores_shape, F32)]
                       + [pltpu.VMEM(scores_shape, F32)] * 2
                       + [pltpu.VMEM(scores_shape, BF16)] * 2
                       + [pltpu.VMEM((N_KV_HEADS, 1, GROUP * BLOCK), F32)] * 2,
        compiler_params=_params("arbitrary"),
        name="win_attn",
    )(rel_bias, q, k4, v4t, bucket_t, sink_rows)


def _mem_kv_kernel(m_ref, w_ref, km_ref, vm_ref):
    mb = m_ref[...].astype(BF16)
    km_ref[...] = _dot(mb, w_ref[:, :MEM_DIM]).astype(BF16)
    vm_ref[...] = _dot(mb, w_ref[:, MEM_DIM:]).astype(BF16)


def _mem_kv(mem2, w_mem_kv, tm):
    t = mem2.shape[0]
    out = pl.BlockSpec((tm, MEM_DIM), lambda i: (i, 0))
    return pl.pallas_call(
        _mem_kv_kernel,
        grid=(t // tm,),
        in_specs=[pl.BlockSpec((tm, D_MODEL), lambda i: (i, 0)), _resident(w_mem_kv.shape)],
        out_specs=[out, out],
        out_shape=[jax.ShapeDtypeStruct((t, MEM_DIM), BF16)] * 2,
        compiler_params=_params("parallel"),
        name="mem_kv",
    )(mem2, w_mem_kv)


def _merge_kernel(alpha, tiles_per_seq, x_ref, u_ref, u_prev_ref, u_next_ref, a_ref, qm_ref,
                  km_ref, vm_ref, cw_ref, cb_ref, cg_ref, cbeta_ref, win_ref, bg_ref,
                  wc_ref, wa_ref, wm_ref, wo_ref, g_ref, b_ref, h_ref, seq_ref, taps_ref):
    x = x_ref[...]
    xb = x.astype(BF16)
    tile_in_seq = pl.program_id(0) % tiles_per_seq
    c = _tile_conv(u_ref, u_prev_ref, u_next_ref, cw_ref, cb_ref, cg_ref, cbeta_ref,
                   tile_in_seq == 0, tile_in_seq == tiles_per_seq - 1, seq_ref, taps_ref)

    mem_scale = MEM_HEAD_DIM ** -0.5
    heads = []
    for h in range(N_MEM_HEADS):
        lanes = slice(h * MEM_HEAD_DIM, (h + 1) * MEM_HEAD_DIM)
        s = lax.dot_general(qm_ref[:, lanes], km_ref[:, lanes], (((1,), (1,)), ((), ())),
                            preferred_element_type=F32) * mem_scale
        p = jnp.exp(s - jnp.max(s, axis=-1, keepdims=True))
        denom = jnp.sum(p, axis=-1, keepdims=True)
        heads.append(_dot(p.astype(BF16), vm_ref[:, lanes]) * (1.0 / denom))
    mo = jnp.concatenate(heads, axis=1).astype(BF16)

    def gate(i):
        cols = slice(i * D_MODEL, (i + 1) * D_MODEL)
        w_cols = slice(PROJ_DIM + i * D_MODEL, PROJ_DIM + (i + 1) * D_MODEL)
        return jax.nn.sigmoid(_dot(xb, win_ref[:, w_cols]) + bg_ref[:, cols])

    merged = gate(0) * _dot(c.astype(BF16), wc_ref[...])
    merged = merged + gate(1) * _dot(a_ref[...], wa_ref[...])
    merged = merged + gate(2) * _dot(mo, wm_ref[...])
    merged = merged.astype(BF16)
    for r in range(0, x_ref.shape[0], _EPILOGUE_ROWS):
        rows = slice(r, r + _EPILOGUE_ROWS)
        out = _dot(merged[rows, :], wo_ref[...])
        h_ref[rows, :] = _layer_norm(alpha * x_ref[rows, :] + out, g_ref[...], b_ref[...])


def _merge(alpha, x2, u, a, qm, km, vm, conv_w, conv_b, conv_g, conv_beta, w_in, bg, wc, wa, wm, wo,
           ln_g, ln_b, seq, tm):
    t = x2.shape[0]
    tiles_per_seq = seq // tm
    groups = tm // V7X_SUBLANES
    edge_blocks = tm // CONV_EDGE_ROWS
    assert groups >= CONV_HALO and groups % _CONV_GROUPS == 0 and CONV_EDGE_ROWS >= CONV_HALO
    row = lambda width: pl.BlockSpec((tm, width), lambda i: (i, 0))
    edge = lambda index_map: pl.BlockSpec((CONV_EDGE_ROWS, CONV_DIM), index_map)
    mem_spec = pl.BlockSpec((N_MEM, MEM_DIM), lambda i: (i // tiles_per_seq, 0))
    weights = (conv_w, conv_b, conv_g, conv_beta, w_in, bg, wc, wa, wm, wo, ln_g, ln_b)
    return pl.pallas_call(
        functools.partial(_merge_kernel, alpha, tiles_per_seq),
        grid=(t // tm,),
        in_specs=[row(D_MODEL),
                  pl.BlockSpec((V7X_SUBLANES, groups, CONV_DIM), lambda i: (i, 0, 0)),
                  edge(lambda i: (jnp.maximum(i * edge_blocks - 1, 0), 0)),
                  edge(lambda i: (jnp.minimum((i + 1) * edge_blocks, t // CONV_EDGE_ROWS - 1), 0)),
                  row(ATTN_Q_DIM), row(MEM_DIM), mem_spec, mem_spec]
                 + [_resident(w.shape) for w in weights],
        out_specs=row(D_MODEL),
        out_shape=jax.ShapeDtypeStruct((t, D_MODEL), F32),
        scratch_shapes=[pltpu.VMEM((groups + 2 * CONV_HALO, V7X_SUBLANES, CONV_DIM), F32),
                        pltpu.VMEM((CONV_WIDTH, V7X_SUBLANES, CONV_DIM), F32)],
        compiler_params=_params("parallel"),
        name="merge",
    )(x2, u.reshape(t // groups, groups, CONV_DIM), u, u, a, qm, km, vm, *weights)


def _ffn_kernel(alpha, chunk, h_ref, w1_ref, w2_ref, g_ref, b_ref, o_ref, act_ref):
    h = h_ref[...]
    hb = h.astype(BF16)
    d_ff = w2_ref.shape[0]
    for lo in range(0, d_ff, chunk):
        gate = _dot(hb, w1_ref[:, lo:lo + chunk])
        up = _dot(hb, w1_ref[:, d_ff + lo:d_ff + lo + chunk])
        act_ref[:, lo:lo + chunk] = (gate * jax.nn.sigmoid(gate) * up).astype(BF16)
    for r in range(0, h_ref.shape[0], _EPILOGUE_ROWS):
        rows = slice(r, r + _EPILOGUE_ROWS)
        ffn = _dot(act_ref[rows, :], w2_ref[...])
        o_ref[rows, :] = _layer_norm(alpha * h_ref[rows, :] + ffn, g_ref[...], b_ref[...])


def _ffn(alpha, h, w1, w2, ln_g, ln_b, tm, chunk):
    t = h.shape[0]
    d_ff = w2.shape[0]
    row = pl.BlockSpec((tm, D_MODEL), lambda i: (i, 0))
    weights = (w1, w2, ln_g, ln_b)
    return pl.pallas_call(
        functools.partial(_ffn_kernel, alpha, chunk),
        grid=(t // tm,),
        in_specs=[row] + [_resident(w.shape) for w in weights],
        out_specs=row,
        out_shape=jax.ShapeDtypeStruct((t, D_MODEL), F32),
        scratch_shapes=[pltpu.VMEM((tm, d_ff), BF16)],
        compiler_params=_params("parallel"),
        name="ffn",
    )(h, *weights)


def _t5_bucket(rel):
    half = N_BUCKETS // 2
    max_exact = half // 2
    base = jnp.where(rel > 0, half, 0)
    n = jnp.abs(rel)
    nf = jnp.maximum(n, 1).astype(jnp.float32)
    large = max_exact + (jnp.log(nf / max_exact) / math.log(MAX_DISTANCE / max_exact)
                         * (half - max_exact)).astype(jnp.int32)
    large = jnp.minimum(large, half - 1)
    return base + jnp.where(n < max_exact, n, large)


def _layer(x2, mem2, bucket, rel_bias, seq, alpha, w_in, b_gate, conv_dw_w, conv_dw_b, conv_ln_g,
           conv_ln_b, w_conv_out, attn_sink, w_attn_out, w_mem_kv, w_mem_out, w_o, ln1_g, ln1_b,
           w_ffn_in, w_ffn_out, ln2_g, ln2_b):
    row = lambda v: v.reshape(1, -1)
    w_in = w_in.astype(BF16)
    u, q, k4, v4t, qm = _in_proj(x2, w_in, tm=512)
    sink_rows = jnp.repeat(attn_sink.astype(F32).reshape(N_KV_HEADS, GROUP), BLOCK, axis=1)[:, None, :]
    a = _win_attn(q, k4, v4t, rel_bias.astype(F32), bucket, sink_rows, seq)
    km, vm = _mem_kv(mem2, w_mem_kv.astype(BF16), tm=512)
    h = _merge(alpha, x2, u, a, qm, km, vm, conv_dw_w, row(conv_dw_b), row(conv_ln_g), row(conv_ln_b),
               w_in, row(b_gate),
               w_conv_out.astype(BF16), w_attn_out.astype(BF16), w_mem_out.astype(BF16),
               w_o.astype(BF16), row(ln1_g), row(ln1_b), seq, tm=512)
    return _ffn(alpha, h, w_ffn_in.astype(BF16), w_ffn_out.astype(BF16), row(ln2_g), row(ln2_b),
                tm=1024, chunk=256)


def kernel(x, mem, rel_bias, w_in, b_gate, conv_dw_w, conv_dw_b, conv_ln_g, conv_ln_b, w_conv_out,
           attn_sink, w_attn_out, w_mem_kv, w_mem_out, w_o, ln1_g, ln1_b, w_ffn_in, w_ffn_out,
           ln2_g, ln2_b):
    batch, seq, d_model = x.shape
    depth = w_in.shape[0]
    assert d_model == D_MODEL and mem.shape[1:] == (N_MEM, D_MODEL) and seq % BLOCK == 0
    alpha = (2 * depth) ** 0.25
    qloc = jnp.arange(BLOCK, dtype=jnp.int32)
    kloc = jnp.arange(3 * BLOCK, dtype=jnp.int32) - BLOCK
    bucket = _t5_bucket(kloc[:, None] - qloc[None, :])
    x2 = x.reshape(batch * seq, d_model)
    mem2 = mem.reshape(batch * N_MEM, d_model)
    per_layer = (w_in, b_gate, conv_dw_w, conv_dw_b, conv_ln_g, conv_ln_b, w_conv_out, attn_sink,
                 w_attn_out, w_mem_kv, w_mem_out, w_o, ln1_g, ln1_b, w_ffn_in, w_ffn_out, ln2_g, ln2_b)
    for l in range(depth):
        x2 = _layer(x2, mem2, bucket, rel_bias, seq, alpha, *(p[l] for p in per_layer))
    return x2.reshape(batch, seq, d_model)
```

```python
import functools
import math

import jax
import jax.numpy as jnp
from jax import lax
from jax.experimental import pallas as pl
from jax.experimental.pallas import tpu as pltpu

D_MODEL = 1024
N_MEM = 256
CONV_DIM = 512
CONV_WIDTH = 31
N_HEADS = 8
N_KV_HEADS = 2
HEAD_DIM = 64
WINDOW = 128
BLOCK = 128
N_MEM_HEADS = 4
MEM_HEAD_DIM = 128
N_BUCKETS = 32
MAX_DISTANCE = 128
N_BRANCHES = 3
LN_EPS = 1e-5
NEG_INF = -1e30

GROUP = N_HEADS // N_KV_HEADS
ATTN_Q_DIM = N_HEADS * HEAD_DIM
KV_DIM = N_KV_HEADS * HEAD_DIM
MEM_DIM = N_MEM_HEADS * MEM_HEAD_DIM
GROUP_DIM = GROUP * HEAD_DIM
ATTN_SCALE = HEAD_DIM ** -0.5
LOG2E = math.log2(math.e)
PROJ_DIM = 2 * CONV_DIM + ATTN_Q_DIM + 2 * KV_DIM + MEM_DIM

V7X_LANES = 128
V7X_SUBLANES = 8
V7X_VMEM_LIMIT_BYTES = 56 * 1024 * 1024

BF16 = jnp.bfloat16
F32 = jnp.float32


def _params(*semantics):
    return pltpu.CompilerParams(dimension_semantics=semantics,
                                vmem_limit_bytes=V7X_VMEM_LIMIT_BYTES)


def _resident(shape):
    nd = len(shape)
    return pl.BlockSpec(shape, lambda *_: (0,) * nd, pipeline_mode=pl.Buffered(1))


def _dot(a, b):
    return jnp.dot(a, b, preferred_element_type=F32)


_EPILOGUE_ROWS = 256


def _layer_norm(v, g, b):
    mu = jnp.mean(v, axis=-1, keepdims=True)
    d = v - mu
    var = jnp.mean(d * d, axis=-1, keepdims=True)
    return d * lax.rsqrt(var + LN_EPS) * g + b


def _in_proj_kernel(x_ref, w_ref, u_ref, q_ref, k4_ref, v4t_ref, qm_ref):
    xb = x_ref[...].astype(BF16)

    def proj(lo, width):
        return _dot(xb, w_ref[:, lo:lo + width])

    a = proj(0, CONV_DIM)
    g = proj(CONV_DIM, CONV_DIM)
    u_ref[...] = a * jax.nn.sigmoid(g)
    off = 2 * CONV_DIM
    q_ref[...] = (proj(off, ATTN_Q_DIM) * (ATTN_SCALE * LOG2E)).astype(BF16)
    off += ATTN_Q_DIM
    k = proj(off, KV_DIM)
    v = proj(off + KV_DIM, KV_DIM)
    off += 2 * KV_DIM
    qm_ref[...] = proj(off, MEM_DIM).astype(BF16)

    first_half = lax.broadcasted_iota(jnp.int32, k.shape, 1) < HEAD_DIM
    swapped = pltpu.roll(k, HEAD_DIM, axis=1)
    h0 = jnp.where(first_half, k, swapped).astype(BF16)
    h1 = jnp.where(first_half, swapped, k).astype(BF16)
    k4_ref[...] = jnp.concatenate([h0, h0, h1, h1], axis=1)

    vt = v.T.astype(BF16)
    for c in range(v4t_ref.shape[0]):
        keys = vt[:, c * BLOCK:(c + 1) * BLOCK]
        v4t_ref[c] = jnp.concatenate([keys[:HEAD_DIM]] * GROUP + [keys[HEAD_DIM:]] * GROUP, axis=0)


def _in_proj(x2, w_in, tm):
    t = x2.shape[0]
    row = lambda width: pl.BlockSpec((tm, width), lambda i: (i, 0))
    kv_tiled = N_KV_HEADS * GROUP_DIM
    out_specs = [row(CONV_DIM), row(ATTN_Q_DIM), row(kv_tiled),
                 pl.BlockSpec((tm // BLOCK, kv_tiled, BLOCK), lambda i: (i, 0, 0)), row(MEM_DIM)]
    out_shape = [jax.ShapeDtypeStruct((t, CONV_DIM), F32),
                 jax.ShapeDtypeStruct((t, ATTN_Q_DIM), BF16),
                 jax.ShapeDtypeStruct((t, kv_tiled), BF16),
                 jax.ShapeDtypeStruct((t // BLOCK, kv_tiled, BLOCK), BF16),
                 jax.ShapeDtypeStruct((t, MEM_DIM), BF16)]
    return pl.pallas_call(
        _in_proj_kernel,
        grid=(t // tm,),
        in_specs=[row(D_MODEL),
                  pl.BlockSpec((D_MODEL, PROJ_DIM), lambda i: (0, 0), pipeline_mode=pl.Buffered(1))],
        out_specs=out_specs,
        out_shape=out_shape,
        compiler_params=_params("parallel"),
        name="in_proj",
    )(x2, w_in)


CONV_HALO = CONV_WIDTH // 2
CONV_EDGE_ROWS = 16
_CONV_GROUPS = 8


def _tile_conv(u_ref, prev_ref, next_ref, w_ref, b_ref, g_ref, beta_ref, is_first, is_last,
               seq_ref, taps_ref):
    sub, n, _ = u_ref.shape
    for k in range(CONV_WIDTH):
        taps_ref[k] = jnp.broadcast_to(w_ref[k:k + 1, :], (sub, CONV_DIM))
    for a0 in range(0, n, sub):
        seq_ref[CONV_HALO + a0:CONV_HALO + a0 + sub] = pltpu.einshape("rac->arc", u_ref[:, a0:a0 + sub, :])

    phase = lax.broadcasted_iota(jnp.int32, (CONV_HALO, sub, CONV_DIM), 1)
    outside_before = jnp.where(is_first, 0.0, prev_ref[CONV_EDGE_ROWS - CONV_HALO:, :])
    before = pltpu.roll(seq_ref[n:n + CONV_HALO], 1, axis=1)
    seq_ref[0:CONV_HALO] = jnp.where(phase == 0, outside_before[:, None, :], before)
    outside_after = jnp.where(is_last, 0.0, next_ref[:CONV_HALO, :])
    after = pltpu.roll(seq_ref[CONV_HALO:2 * CONV_HALO], sub - 1, axis=1)
    seq_ref[CONV_HALO + n:2 * CONV_HALO + n] = jnp.where(phase == sub - 1, outside_after[:, None, :], after)

    done = []
    for a0 in range(0, n, _CONV_GROUPS):
        acc = jnp.zeros((_CONV_GROUPS, sub, CONV_DIM), F32)
        for k in range(CONV_WIDTH):
            acc = acc + seq_ref[a0 + k:a0 + k + _CONV_GROUPS] * taps_ref[k]
        y = _layer_norm(acc + b_ref[...], g_ref[...], beta_ref[...])
        done.append(y * jax.nn.sigmoid(y))
    y = pltpu.einshape("arc->rac", jnp.concatenate(done, axis=0))
    return y.reshape(sub * n, CONV_DIM)


def _win_attn_kernel(rb_ref, q_ref, k4_ref, v4t_ref, bucket_ref, sink_ref, o_ref,
                     bias_ref, s0_ref, s1_ref, p0_ref, p1_ref, inv0_ref, inv1_ref):
    s_ref, p_ref, inv_ref = (s0_ref, s1_ref), (p0_ref, p1_ref), (inv0_ref, inv1_ref)
    n_blocks = v4t_ref.shape[0]
    last = n_blocks - 1
    keys = 3 * BLOCK
    first_var, mid_var, last_var = 0, 1, 2

    @pl.when(pl.program_id(0) == 0)
    def _():
        bucket = bucket_ref[...]
        kpos = lax.broadcasted_iota(jnp.int32, (keys, BLOCK), 0) - BLOCK
        qpos = lax.broadcasted_iota(jnp.int32, (keys, BLOCK), 1)
        in_band = jnp.abs(kpos - qpos) <= WINDOW
        for h in range(N_HEADS):
            acc = jnp.zeros((keys, BLOCK), F32)
            for bkt in range(N_BUCKETS):
                acc = jnp.where(bucket == bkt, rb_ref[bkt, h] * LOG2E, acc)
            acc = jnp.where(in_band, acc, NEG_INF)
            j, g = divmod(h, GROUP)
            cols = slice(g * BLOCK, (g + 1) * BLOCK)
            bias_ref[mid_var, j, :, cols] = acc
            bias_ref[first_var, j, :, cols] = jnp.where(kpos >= 0, acc, NEG_INF)
            bias_ref[last_var, j, :, cols] = jnp.where(kpos < BLOCK, acc, NEG_INF)

    head_of_lane = lax.broadcasted_iota(jnp.int32, (BLOCK, GROUP_DIM), 1) // HEAD_DIM

    def window(n):
        return [jnp.maximum(n - 1, 0), n, jnp.minimum(n + 1, last)]

    def rows_of(n):
        return pl.ds(pl.multiple_of(n * BLOCK, BLOCK), BLOCK)

    halves = [(j, h) for j in range(N_KV_HEADS) for h in range(2)]
    half_cols = GROUP * BLOCK // 2

    def scores_chunks(n, slot):
        def chunk(j, h):
            lanes = slice(j * GROUP_DIM, (j + 1) * GROUP_DIM)
            qg = q_ref[rows_of(n), lanes]
            qm = jnp.concatenate([jnp.where(head_of_lane == g, qg, jnp.zeros_like(qg))
                                  for g in (2 * h, 2 * h + 1)], axis=0)
            kwin = jnp.concatenate([k4_ref[rows_of(b), lanes] for b in window(n)], axis=0)
            s_ref[slot][j, :, h * half_cols:(h + 1) * half_cols] = lax.dot_general(
                kwin, qm, (((1,), (1,)), ((), ())), preferred_element_type=F32)
        return [functools.partial(chunk, j, h) for j, h in halves]

    def softmax_chunks(slot, var):
        def chunk(j, h):
            for g in (2 * h, 2 * h + 1):
                cols = slice(g * BLOCK, (g + 1) * BLOCK)
                s = s_ref[slot][j, :, cols] + bias_ref[var, j, :, cols]
                sink = sink_ref[j, :, cols] * LOG2E
                m = jnp.maximum(jnp.max(s, axis=0, keepdims=True), sink)
                p = jnp.exp2(s - m)
                denom = jnp.sum(p, axis=0, keepdims=True) + jnp.exp2(sink - m)
                p_ref[slot][j, :, cols] = p.astype(BF16)
                inv_ref[slot][j, :, cols] = 1.0 / denom
        return [functools.partial(chunk, j, h) for j, h in halves]

    def values_chunks(n, slot):
        heads = []

        def chunk(j, h):
            lanes = slice(j * GROUP_DIM, (j + 1) * GROUP_DIM)
            vwin = jnp.concatenate([v4t_ref[b, lanes, :] for b in window(n)], axis=1)
            pv = _dot(vwin, p_ref[slot][j, :, h * half_cols:(h + 1) * half_cols])
            for i, g in enumerate((2 * h, 2 * h + 1)):
                inv = inv_ref[slot][j, :, g * BLOCK:(g + 1) * BLOCK]
                heads.append(pv[g * HEAD_DIM:(g + 1) * HEAD_DIM, i * BLOCK:(i + 1) * BLOCK] * inv)
            if len(heads) == N_HEADS:
                o_ref[rows_of(n), :] = jnp.concatenate(heads, axis=0).T.astype(BF16)
        return [functools.partial(chunk, j, h) for j, h in halves]

    def issue(*stages):
        for chunks in zip(*stages):
            for chunk in chunks:
                chunk()

    def two_steps(i, carry):
        for slot in range(2):
            t = 2 * i + slot
            issue(values_chunks(t - 2, slot), softmax_chunks(1 - slot, mid_var), scores_chunks(t, slot))
        return carry

    issue(scores_chunks(0, 0))
    issue(softmax_chunks(0, first_var), scores_chunks(1, 1))
    lax.fori_loop(1, n_blocks // 2, two_steps, 0)
    issue(values_chunks(last - 1, 0), softmax_chunks(1, last_var))
    issue(values_chunks(last, 1))


def _win_attn(q, k4, v4t, rel_bias, bucket_t, sink_rows, seq):
    t = q.shape[0]
    n_blocks = seq // BLOCK
    assert n_blocks >= 2 and n_blocks % 2 == 0
    scores_shape = (N_KV_HEADS, 3 * BLOCK, GROUP * BLOCK)
    kv_tiled = N_KV_HEADS * GROUP_DIM
    seq_rows = lambda width: pl.BlockSpec((seq, width), lambda b: (b, 0))
    return pl.pallas_call(
        _win_attn_kernel,
        grid=(t // seq,),
        in_specs=[pl.BlockSpec(memory_space=pltpu.SMEM),
                  seq_rows(ATTN_Q_DIM), seq_rows(kv_tiled),
                  pl.BlockSpec((n_blocks, kv_tiled, BLOCK), lambda b: (b, 0, 0)),
                  _resident(bucket_t.shape), _resident(sink_rows.shape)],
        out_specs=seq_rows(ATTN_Q_DIM),
        out_shape=jax.ShapeDtypeStruct((t, ATTN_Q_DIM), BF16),
        scratch_shapes=[pltpu.VMEM((3,) + scores_shape, F32)]
                       + [pltpu.VMEM(scores_shape, F32)] * 2
                       + [pltpu.VMEM(scores_shape, BF16)] * 2
                       + [pltpu.VMEM((N_KV_HEADS, 1, GROUP * BLOCK), F32)] * 2,
        compiler_params=_params("arbitrary"),
        name="win_attn",
    )(rel_bias, q, k4, v4t, bucket_t, sink_rows)


def _attn_bias_init(rb_ref, bucket_ref, bias_ref):
    keys = 3 * BLOCK
    bucket = bucket_ref[...]
    kpos = lax.broadcasted_iota(jnp.int32, (keys, BLOCK), 0) - BLOCK
    qpos = lax.broadcasted_iota(jnp.int32, (keys, BLOCK), 1)
    in_band = jnp.abs(kpos - qpos) <= WINDOW
    for h in range(N_HEADS):
        acc = jnp.zeros((keys, BLOCK), F32)
        for bkt in range(N_BUCKETS):
            acc = jnp.where(bucket == bkt, rb_ref[bkt, h] * LOG2E, acc)
        j, g = divmod(h, GROUP)
        bias_ref[j, :, g * BLOCK:(g + 1) * BLOCK] = jnp.where(in_band, acc, NEG_INF)


def _tile_attention(q_ref, k4_ref, v4t_ref, bias_ref, sink_ref, first_block):
    last = v4t_ref.shape[0] - 1
    blocks_per_tile = q_ref.shape[0] // BLOCK
    head_of_lane = lax.broadcasted_iota(jnp.int32, (BLOCK, GROUP_DIM), 1) // HEAD_DIM
    rows_of = lambda b: pl.ds(pl.multiple_of(b * BLOCK, BLOCK), BLOCK)
    outs = []
    for bi in range(blocks_per_tile):
        n = first_block + bi
        window = [jnp.maximum(n - 1, 0), n, jnp.minimum(n + 1, last)]
        heads = []
        for j in range(N_KV_HEADS):
            lanes = slice(j * GROUP_DIM, (j + 1) * GROUP_DIM)
            qg = q_ref[bi * BLOCK:(bi + 1) * BLOCK, lanes]
            qm = jnp.concatenate([jnp.where(head_of_lane == g, qg, jnp.zeros_like(qg))
                                  for g in range(GROUP)], axis=0)
            kwin = jnp.concatenate([k4_ref[rows_of(b), lanes] for b in window], axis=0)
            vwin = jnp.concatenate([v4t_ref[b, lanes, :] for b in window], axis=1)
            s = lax.dot_general(kwin, qm, (((1,), (1,)), ((), ())), preferred_element_type=F32)
            s = s + bias_ref[j]
            before = jnp.where(n > 0, s[:BLOCK], NEG_INF) if bi == 0 else s[:BLOCK]
            after = jnp.where(n < last, s[2 * BLOCK:], NEG_INF) if bi == blocks_per_tile - 1 else s[2 * BLOCK:]
            s = jnp.concatenate([before, s[BLOCK:2 * BLOCK], after], axis=0)
            sink = sink_ref[j] * LOG2E
            m = jnp.maximum(jnp.max(s, axis=0, keepdims=True), sink)
            p = jnp.exp2(s - m)
            inv = 1.0 / (jnp.sum(p, axis=0, keepdims=True) + jnp.exp2(sink - m))
            pv = _dot(vwin, p.astype(BF16))
            for g in range(GROUP):
                cols = slice(g * BLOCK, (g + 1) * BLOCK)
                heads.append(pv[g * HEAD_DIM:(g + 1) * HEAD_DIM, cols] * inv[:, cols])
        outs.append(jnp.concatenate(heads, axis=0).T)
    return jnp.concatenate(outs, axis=0)


def _mem_kv_kernel(m_ref, w_ref, km_ref, vm_ref):
    mb = m_ref[...].astype(BF16)
    km_ref[...] = _dot(mb, w_ref[:, :MEM_DIM]).astype(BF16)
    vm_ref[...] = _dot(mb, w_ref[:, MEM_DIM:]).astype(BF16)


def _mem_kv(mem2, w_mem_kv, tm):
    t = mem2.shape[0]
    out = pl.BlockSpec((tm, MEM_DIM), lambda i: (i, 0))
    return pl.pallas_call(
        _mem_kv_kernel,
        grid=(t // tm,),
        in_specs=[pl.BlockSpec((tm, D_MODEL), lambda i: (i, 0)), _resident(w_mem_kv.shape)],
        out_specs=[out, out],
        out_shape=[jax.ShapeDtypeStruct((t, MEM_DIM), BF16)] * 2,
        compiler_params=_params("parallel"),
        name="mem_kv",
    )(mem2, w_mem_kv)


def _merge_kernel(alpha, tiles_per_seq, rb_ref, x_ref, u_ref, u_prev_ref, u_next_ref, q_ref, k4_ref,
                  v4t_ref, qm_ref, km_ref, vm_ref, bucket_ref, sink_ref, cw_ref, cb_ref, cg_ref,
                  cbeta_ref, win_ref, bg_ref, wc_ref, wa_ref, wm_ref, wo_ref, g_ref, b_ref, h_ref,
                  seq_ref, taps_ref, bias_ref):
    @pl.when(pl.program_id(0) == 0)
    def _():
        _attn_bias_init(rb_ref, bucket_ref, bias_ref)

    x = x_ref[...]
    xb = x.astype(BF16)
    tile_in_seq = pl.program_id(0) % tiles_per_seq
    c = _tile_conv(u_ref, u_prev_ref, u_next_ref, cw_ref, cb_ref, cg_ref, cbeta_ref,
                   tile_in_seq == 0, tile_in_seq == tiles_per_seq - 1, seq_ref, taps_ref)
    a = _tile_attention(q_ref, k4_ref, v4t_ref, bias_ref, sink_ref,
                        tile_in_seq * (q_ref.shape[0] // BLOCK))

    mem_scale = MEM_HEAD_DIM ** -0.5
    heads = []
    for h in range(N_MEM_HEADS):
        lanes = slice(h * MEM_HEAD_DIM, (h + 1) * MEM_HEAD_DIM)
        s = lax.dot_general(qm_ref[:, lanes], km_ref[:, lanes], (((1,), (1,)), ((), ())),
                            preferred_element_type=F32) * mem_scale
        p = jnp.exp(s - jnp.max(s, axis=-1, keepdims=True))
        denom = jnp.sum(p, axis=-1, keepdims=True)
        heads.append(_dot(p.astype(BF16), vm_ref[:, lanes]) * (1.0 / denom))
    mo = jnp.concatenate(heads, axis=1).astype(BF16)

    def gate(i):
        cols = slice(i * D_MODEL, (i + 1) * D_MODEL)
        w_cols = slice(PROJ_DIM + i * D_MODEL, PROJ_DIM + (i + 1) * D_MODEL)
        return jax.nn.sigmoid(_dot(xb, win_ref[:, w_cols]) + bg_ref[:, cols])

    merged = gate(0) * _dot(c.astype(BF16), wc_ref[...])
    merged = merged + gate(1) * _dot(a.astype(BF16), wa_ref[...])
    merged = merged + gate(2) * _dot(mo, wm_ref[...])
    merged = merged.astype(BF16)
    for r in range(0, x_ref.shape[0], _EPILOGUE_ROWS):
        rows = slice(r, r + _EPILOGUE_ROWS)
        out = _dot(merged[rows, :], wo_ref[...])
        h_ref[rows, :] = _layer_norm(alpha * x_ref[rows, :] + out, g_ref[...], b_ref[...])


def _merge(alpha, x2, u, q, k4, v4t, qm, km, vm, rel_bias, bucket_t, sink_rows, conv_w, conv_b, conv_g,
           conv_beta, w_in, bg, wc, wa, wm, wo, ln_g, ln_b, seq, tm):
    t = x2.shape[0]
    tiles_per_seq = seq // tm
    n_blocks = seq // BLOCK
    kv_tiled = N_KV_HEADS * GROUP_DIM
    groups = tm // V7X_SUBLANES
    edge_blocks = tm // CONV_EDGE_ROWS
    assert groups >= CONV_HALO and groups % _CONV_GROUPS == 0 and CONV_EDGE_ROWS >= CONV_HALO
    assert tm % BLOCK == 0 and n_blocks >= 2
    row = lambda width: pl.BlockSpec((tm, width), lambda i: (i, 0))
    edge = lambda index_map: pl.BlockSpec((CONV_EDGE_ROWS, CONV_DIM), index_map)
    mem_spec = pl.BlockSpec((N_MEM, MEM_DIM), lambda i: (i // tiles_per_seq, 0))
    weights = (conv_w, conv_b, conv_g, conv_beta, w_in, bg, wc, wa, wm, wo, ln_g, ln_b)
    return pl.pallas_call(
        functools.partial(_merge_kernel, alpha, tiles_per_seq),
        grid=(t // tm,),
        in_specs=[pl.BlockSpec(memory_space=pltpu.SMEM),
                  row(D_MODEL),
                  pl.BlockSpec((V7X_SUBLANES, groups, CONV_DIM), lambda i: (i, 0, 0)),
                  edge(lambda i: (jnp.maximum(i * edge_blocks - 1, 0), 0)),
                  edge(lambda i: (jnp.minimum((i + 1) * edge_blocks, t // CONV_EDGE_ROWS - 1), 0)),
                  row(ATTN_Q_DIM),
                  pl.BlockSpec((seq, kv_tiled), lambda i: (i // tiles_per_seq, 0)),
                  pl.BlockSpec((n_blocks, kv_tiled, BLOCK), lambda i: (i // tiles_per_seq, 0, 0)),
                  row(MEM_DIM), mem_spec, mem_spec,
                  _resident(bucket_t.shape), _resident(sink_rows.shape)]
                 + [_resident(w.shape) for w in weights],
        out_specs=row(D_MODEL),
        out_shape=jax.ShapeDtypeStruct((t, D_MODEL), F32),
        scratch_shapes=[pltpu.VMEM((groups + 2 * CONV_HALO, V7X_SUBLANES, CONV_DIM), F32),
                        pltpu.VMEM((CONV_WIDTH, V7X_SUBLANES, CONV_DIM), F32),
                        pltpu.VMEM((N_KV_HEADS, 3 * BLOCK, GROUP * BLOCK), F32)],
        compiler_params=_params("arbitrary"),
        name="merge",
    )(rel_bias, x2, u.reshape(t // groups, groups, CONV_DIM), u, u, q, k4, v4t, qm, km, vm,
      bucket_t, sink_rows, *weights)


def _ffn_kernel(alpha, chunk, h_ref, w1_ref, w2_ref, g_ref, b_ref, o_ref, act_ref):
    h = h_ref[...]
    hb = h.astype(BF16)
    d_ff = w2_ref.shape[0]
    for lo in range(0, d_ff, chunk):
        gate = _dot(hb, w1_ref[:, lo:lo + chunk])
        up = _dot(hb, w1_ref[:, d_ff + lo:d_ff + lo + chunk])
        act_ref[:, lo:lo + chunk] = (gate * jax.nn.sigmoid(gate) * up).astype(BF16)
    for r in range(0, h_ref.shape[0], _EPILOGUE_ROWS):
        rows = slice(r, r + _EPILOGUE_ROWS)
        ffn = _dot(act_ref[rows, :], w2_ref[...])
        o_ref[rows, :] = _layer_norm(alpha * h_ref[rows, :] + ffn, g_ref[...], b_ref[...])


def _ffn(alpha, h, w1, w2, ln_g, ln_b, tm, chunk):
    t = h.shape[0]
    d_ff = w2.shape[0]
    row = pl.BlockSpec((tm, D_MODEL), lambda i: (i, 0))
    weights = (w1, w2, ln_g, ln_b)
    return pl.pallas_call(
        functools.partial(_ffn_kernel, alpha, chunk),
        grid=(t // tm,),
        in_specs=[row] + [_resident(w.shape) for w in weights],
        out_specs=row,
        out_shape=jax.ShapeDtypeStruct((t, D_MODEL), F32),
        scratch_shapes=[pltpu.VMEM((tm, d_ff), BF16)],
        compiler_params=_params("parallel"),
        name="ffn",
    )(h, *weights)


def _t5_bucket(rel):
    half = N_BUCKETS // 2
    max_exact = half // 2
    base = jnp.where(rel > 0, half, 0)
    n = jnp.abs(rel)
    nf = jnp.maximum(n, 1).astype(jnp.float32)
    large = max_exact + (jnp.log(nf / max_exact) / math.log(MAX_DISTANCE / max_exact)
                         * (half - max_exact)).astype(jnp.int32)
    large = jnp.minimum(large, half - 1)
    return base + jnp.where(n < max_exact, n, large)


def _layer(x2, mem2, bucket, rel_bias, seq, alpha, w_in, b_gate, conv_dw_w, conv_dw_b, conv_ln_g,
           conv_ln_b, w_conv_out, attn_sink, w_attn_out, w_mem_kv, w_mem_out, w_o, ln1_g, ln1_b,
           w_ffn_in, w_ffn_out, ln2_g, ln2_b):
    row = lambda v: v.reshape(1, -1)
    w_in = w_in.astype(BF16)
    u, q, k4, v4t, qm = _in_proj(x2, w_in, tm=512)
    sink_rows = jnp.repeat(attn_sink.astype(F32).reshape(N_KV_HEADS, GROUP), BLOCK, axis=1)[:, None, :]
    km, vm = _mem_kv(mem2, w_mem_kv.astype(BF16), tm=512)
    h = _merge(alpha, x2, u, q, k4, v4t, qm, km, vm, rel_bias.astype(F32), bucket, sink_rows,
               conv_dw_w, row(conv_dw_b), row(conv_ln_g), row(conv_ln_b),
               w_in, row(b_gate),
               w_conv_out.astype(BF16), w_attn_out.astype(BF16), w_mem_out.astype(BF16),
               w_o.astype(BF16), row(ln1_g), row(ln1_b), seq, tm=512)
    return _ffn(alpha, h, w_ffn_in.astype(BF16), w_ffn_out.astype(BF16), row(ln2_g), row(ln2_b),
                tm=1024, chunk=256)


def kernel(x, mem, rel_bias, w_in, b_gate, conv_dw_w, conv_dw_b, conv_ln_g, conv_ln_b, w_conv_out,
           attn_sink, w_attn_out, w_mem_kv, w_mem_out, w_o, ln1_g, ln1_b, w_ffn_in, w_ffn_out,
           ln2_g, ln2_b):
    batch, seq, d_model = x.shape
    depth = w_in.shape[0]
    assert d_model == D_MODEL and mem.shape[1:] == (N_MEM, D_MODEL) and seq % BLOCK == 0
    alpha = (2 * depth) ** 0.25
    qloc = jnp.arange(BLOCK, dtype=jnp.int32)
    kloc = jnp.arange(3 * BLOCK, dtype=jnp.int32) - BLOCK
    bucket = _t5_bucket(kloc[:, None] - qloc[None, :])
    x2 = x.reshape(batch * seq, d_model)
    mem2 = mem.reshape(batch * N_MEM, d_model)
    per_layer = (w_in, b_gate, conv_dw_w, conv_dw_b, conv_ln_g, conv_ln_b, w_conv_out, attn_sink,
                 w_attn_out, w_mem_kv, w_mem_out, w_o, ln1_g, ln1_b, w_ffn_in, w_ffn_out, ln2_g, ln2_b)
    for l in range(depth):
        x2 = _layer(x2, mem2, bucket, rel_bias, seq, alpha, *(p[l] for p in per_layer))
    return x2.reshape(batch, seq, d_model)
```

```python
import functools
import math

import jax
import jax.numpy as jnp
from jax import lax
from jax.experimental import pallas as pl
from jax.experimental.pallas import tpu as pltpu

D_MODEL = 1024
N_MEM = 256
CONV_DIM = 512
CONV_WIDTH = 31
N_HEADS = 8
N_KV_HEADS = 2
HEAD_DIM = 64
WINDOW = 128
BLOCK = 128
N_MEM_HEADS = 4
MEM_HEAD_DIM = 128
N_BUCKETS = 32
MAX_DISTANCE = 128
N_BRANCHES = 3
LN_EPS = 1e-5
NEG_INF = -1e30

GROUP = N_HEADS // N_KV_HEADS
ATTN_Q_DIM = N_HEADS * HEAD_DIM
KV_DIM = N_KV_HEADS * HEAD_DIM
MEM_DIM = N_MEM_HEADS * MEM_HEAD_DIM
GROUP_DIM = GROUP * HEAD_DIM
ATTN_SCALE = HEAD_DIM ** -0.5
LOG2E = math.log2(math.e)
PROJ_DIM = 2 * CONV_DIM + ATTN_Q_DIM + 2 * KV_DIM + MEM_DIM

V7X_LANES = 128
V7X_SUBLANES = 8
V7X_BF16_SUBLANES = 16
V7X_VMEM_LIMIT_BYTES = 56 * 1024 * 1024

BF16 = jnp.bfloat16
F32 = jnp.float32


def _params(*semantics):
    return pltpu.CompilerParams(dimension_semantics=semantics,
                                vmem_limit_bytes=V7X_VMEM_LIMIT_BYTES)


def _resident(shape):
    nd = len(shape)
    return pl.BlockSpec(shape, lambda *_: (0,) * nd, pipeline_mode=pl.Buffered(1))


def _dot(a, b):
    return jnp.dot(a, b, preferred_element_type=F32)


_EPILOGUE_ROWS = 256


def _layer_norm(v, g, b):
    mu = jnp.mean(v, axis=-1, keepdims=True)
    d = v - mu
    var = jnp.mean(d * d, axis=-1, keepdims=True)
    return d * lax.rsqrt(var + LN_EPS) * g + b


def _cast_chunks(rows, n_steps):
    return max(n for n in range(1, n_steps + 1)
               if rows % n == 0 and (rows // n) % V7X_BF16_SUBLANES == 0)


def _cast_plan(weights, n_steps):
    chunks = tuple(_cast_chunks(w.shape[0], n_steps) for w in weights)
    specs = [pl.BlockSpec((w.shape[0] // n, w.shape[1]), lambda i, n=n: (jnp.minimum(i, n - 1), 0))
             for w, n in zip(weights, chunks)]
    shapes = [jax.ShapeDtypeStruct(w.shape, BF16) for w in weights]
    return chunks, specs, shapes


def _cast_side_job(chunks, src_refs, dst_refs):
    step = pl.program_id(0)
    for n, src, dst in zip(chunks, src_refs, dst_refs):
        @pl.when(step < n)
        def _():
            dst[...] = src[...].astype(BF16)


def _in_proj_kernel(chunks, x_ref, w_ref, *refs):
    n_cast = len(chunks)
    cast_srcs, refs = refs[:n_cast], refs[n_cast:]
    u_ref, q_ref, k4_ref, v4t_ref, qm_ref = refs[:5]
    cast_dsts, (wb_ref,) = refs[5:5 + n_cast], refs[5 + n_cast:]

    @pl.when(pl.program_id(0) == 0)
    def _():
        wb_ref[...] = w_ref[...].astype(BF16)

    _cast_side_job(chunks, cast_srcs, cast_dsts)
    xb = x_ref[...].astype(BF16)

    def proj(lo, width):
        return _dot(xb, wb_ref[:, lo:lo + width])

    a = proj(0, CONV_DIM)
    g = proj(CONV_DIM, CONV_DIM)
    u_ref[...] = a * jax.nn.sigmoid(g)
    off = 2 * CONV_DIM
    q_ref[...] = (proj(off, ATTN_Q_DIM) * (ATTN_SCALE * LOG2E)).astype(BF16)
    off += ATTN_Q_DIM
    k = proj(off, KV_DIM)
    v = proj(off + KV_DIM, KV_DIM)
    off += 2 * KV_DIM
    qm_ref[...] = proj(off, MEM_DIM).astype(BF16)

    first_half = lax.broadcasted_iota(jnp.int32, k.shape, 1) < HEAD_DIM
    swapped = pltpu.roll(k, HEAD_DIM, axis=1)
    h0 = jnp.where(first_half, k, swapped).astype(BF16)
    h1 = jnp.where(first_half, swapped, k).astype(BF16)
    k4_ref[...] = jnp.concatenate([h0, h0, h1, h1], axis=1)

    vt = v.T.astype(BF16)
    for c in range(v4t_ref.shape[0]):
        keys = vt[:, c * BLOCK:(c + 1) * BLOCK]
        v4t_ref[c] = jnp.concatenate([keys[:HEAD_DIM]] * GROUP + [keys[HEAD_DIM:]] * GROUP, axis=0)


def _in_proj(x2, w_in, to_cast, tm):
    t = x2.shape[0]
    n_steps = t // tm
    row = lambda width: pl.BlockSpec((tm, width), lambda i: (i, 0))
    kv_tiled = N_KV_HEADS * GROUP_DIM
    chunks, cast_specs, cast_shapes = _cast_plan(to_cast, n_steps)
    out_specs = [row(CONV_DIM), row(ATTN_Q_DIM), row(kv_tiled),
                 pl.BlockSpec((tm // BLOCK, kv_tiled, BLOCK), lambda i: (i, 0, 0)), row(MEM_DIM)]
    out_shape = [jax.ShapeDtypeStruct((t, CONV_DIM), F32),
                 jax.ShapeDtypeStruct((t, ATTN_Q_DIM), BF16),
                 jax.ShapeDtypeStruct((t, kv_tiled), BF16),
                 jax.ShapeDtypeStruct((t // BLOCK, kv_tiled, BLOCK), BF16),
                 jax.ShapeDtypeStruct((t, MEM_DIM), BF16)]
    outs = pl.pallas_call(
        functools.partial(_in_proj_kernel, chunks),
        grid=(n_steps,),
        in_specs=[row(D_MODEL),
                  pl.BlockSpec((D_MODEL, PROJ_DIM), lambda i: (0, 0), pipeline_mode=pl.Buffered(1))]
                 + cast_specs,
        out_specs=out_specs + cast_specs,
        out_shape=out_shape + cast_shapes,
        scratch_shapes=[pltpu.VMEM((D_MODEL, PROJ_DIM), BF16)],
        compiler_params=_params("arbitrary"),
        name="in_proj",
    )(x2, w_in, *to_cast)
    return outs[:5], outs[5:]


CONV_HALO = CONV_WIDTH // 2
CONV_EDGE_ROWS = 16
_CONV_GROUPS = 8


def _tile_conv(u_ref, prev_ref, next_ref, w_ref, b_ref, g_ref, beta_ref, is_first, is_last,
               seq_ref, taps_ref):
    sub, n, _ = u_ref.shape
    for k in range(CONV_WIDTH):
        taps_ref[k] = jnp.broadcast_to(w_ref[k:k + 1, :], (sub, CONV_DIM))
    for a0 in range(0, n, sub):
        seq_ref[CONV_HALO + a0:CONV_HALO + a0 + sub] = pltpu.einshape("rac->arc", u_ref[:, a0:a0 + sub, :])

    phase = lax.broadcasted_iota(jnp.int32, (CONV_HALO, sub, CONV_DIM), 1)
    outside_before = jnp.where(is_first, 0.0, prev_ref[CONV_EDGE_ROWS - CONV_HALO:, :])
    before = pltpu.roll(seq_ref[n:n + CONV_HALO], 1, axis=1)
    seq_ref[0:CONV_HALO] = jnp.where(phase == 0, outside_before[:, None, :], before)
    outside_after = jnp.where(is_last, 0.0, next_ref[:CONV_HALO, :])
    after = pltpu.roll(seq_ref[CONV_HALO:2 * CONV_HALO], sub - 1, axis=1)
    seq_ref[CONV_HALO + n:2 * CONV_HALO + n] = jnp.where(phase == sub - 1, outside_after[:, None, :], after)

    done = []
    for a0 in range(0, n, _CONV_GROUPS):
        acc = jnp.zeros((_CONV_GROUPS, sub, CONV_DIM), F32)
        for k in range(CONV_WIDTH):
            acc = acc + seq_ref[a0 + k:a0 + k + _CONV_GROUPS] * taps_ref[k]
        y = _layer_norm(acc + b_ref[...], g_ref[...], beta_ref[...])
        done.append(y * jax.nn.sigmoid(y))
    y = pltpu.einshape("arc->rac", jnp.concatenate(done, axis=0))
    return y.reshape(sub * n, CONV_DIM)


def _win_attn_kernel(rb_ref, q_ref, k4_ref, v4t_ref, bucket_ref, sink_ref, o_ref,
                     bias_ref, s0_ref, s1_ref, p0_ref, p1_ref, inv0_ref, inv1_ref):
    s_ref, p_ref, inv_ref = (s0_ref, s1_ref), (p0_ref, p1_ref), (inv0_ref, inv1_ref)
    n_blocks = v4t_ref.shape[0]
    last = n_blocks - 1
    keys = 3 * BLOCK
    first_var, mid_var, last_var = 0, 1, 2

    @pl.when(pl.program_id(0) == 0)
    def _():
        bucket = bucket_ref[...]
        kpos = lax.broadcasted_iota(jnp.int32, (keys, BLOCK), 0) - BLOCK
        qpos = lax.broadcasted_iota(jnp.int32, (keys, BLOCK), 1)
        in_band = jnp.abs(kpos - qpos) <= WINDOW
        for h in range(N_HEADS):
            acc = jnp.zeros((keys, BLOCK), F32)
            for bkt in range(N_BUCKETS):
                acc = jnp.where(bucket == bkt, rb_ref[bkt, h] * LOG2E, acc)
            acc = jnp.where(in_band, acc, NEG_INF)
            j, g = divmod(h, GROUP)
            cols = slice(g * BLOCK, (g + 1) * BLOCK)
            bias_ref[mid_var, j, :, cols] = acc
            bias_ref[first_var, j, :, cols] = jnp.where(kpos >= 0, acc, NEG_INF)
            bias_ref[last_var, j, :, cols] = jnp.where(kpos < BLOCK, acc, NEG_INF)

    head_of_lane = lax.broadcasted_iota(jnp.int32, (BLOCK, GROUP_DIM), 1) // HEAD_DIM

    def window(n):
        return [jnp.maximum(n - 1, 0), n, jnp.minimum(n + 1, last)]

    def rows_of(n):
        return pl.ds(pl.multiple_of(n * BLOCK, BLOCK), BLOCK)

    halves = [(j, h) for j in range(N_KV_HEADS) for h in range(2)]
    half_cols = GROUP * BLOCK // 2

    def scores_chunks(n, slot):
        def chunk(j, h):
            lanes = slice(j * GROUP_DIM, (j + 1) * GROUP_DIM)
            qg = q_ref[rows_of(n), lanes]
            qm = jnp.concatenate([jnp.where(head_of_lane == g, qg, jnp.zeros_like(qg))
                                  for g in (2 * h, 2 * h + 1)], axis=0)
            kwin = jnp.concatenate([k4_ref[rows_of(b), lanes] for b in window(n)], axis=0)
            s_ref[slot][j, :, h * half_cols:(h + 1) * half_cols] = lax.dot_general(
                kwin, qm, (((1,), (1,)), ((), ())), preferred_element_type=F32)
        return [functools.partial(chunk, j, h) for j, h in halves]

    def softmax_chunks(slot, var):
        def chunk(j, h):
            for g in (2 * h, 2 * h + 1):
                cols = slice(g * BLOCK, (g + 1) * BLOCK)
                s = s_ref[slot][j, :, cols] + bias_ref[var, j, :, cols]
                sink = sink_ref[j, :, cols] * LOG2E
                m = jnp.maximum(jnp.max(s, axis=0, keepdims=True), sink)
                p = jnp.exp2(s - m)
                denom = jnp.sum(p, axis=0, keepdims=True) + jnp.exp2(sink - m)
                p_ref[slot][j, :, cols] = p.astype(BF16)
                inv_ref[slot][j, :, cols] = 1.0 / denom
        return [functools.partial(chunk, j, h) for j, h in halves]

    def values_chunks(n, slot):
        heads = []

        def chunk(j, h):
            lanes = slice(j * GROUP_DIM, (j + 1) * GROUP_DIM)
            vwin = jnp.concatenate([v4t_ref[b, lanes, :] for b in window(n)], axis=1)
            pv = _dot(vwin, p_ref[slot][j, :, h * half_cols:(h + 1) * half_cols])
            for i, g in enumerate((2 * h, 2 * h + 1)):
                inv = inv_ref[slot][j, :, g * BLOCK:(g + 1) * BLOCK]
                heads.append(pv[g * HEAD_DIM:(g + 1) * HEAD_DIM, i * BLOCK:(i + 1) * BLOCK] * inv)
            if len(heads) == N_HEADS:
                o_ref[rows_of(n), :] = jnp.concatenate(heads, axis=0).T.astype(BF16)
        return [functools.partial(chunk, j, h) for j, h in halves]

    def issue(*stages):
        for chunks in zip(*stages):
            for chunk in chunks:
                chunk()

    def two_steps(i, carry):
        for slot in range(2):
            t = 2 * i + slot
            issue(values_chunks(t - 2, slot), softmax_chunks(1 - slot, mid_var), scores_chunks(t, slot))
        return carry

    issue(scores_chunks(0, 0))
    issue(softmax_chunks(0, first_var), scores_chunks(1, 1))
    lax.fori_loop(1, n_blocks // 2, two_steps, 0)
    issue(values_chunks(last - 1, 0), softmax_chunks(1, last_var))
    issue(values_chunks(last, 1))


def _win_attn(q, k4, v4t, rel_bias, bucket_t, sink_rows, seq):
    t = q.shape[0]
    n_blocks = seq // BLOCK
    assert n_blocks >= 2 and n_blocks % 2 == 0
    scores_shape = (N_KV_HEADS, 3 * BLOCK, GROUP * BLOCK)
    kv_tiled = N_KV_HEADS * GROUP_DIM
    seq_rows = lambda width: pl.BlockSpec((seq, width), lambda b: (b, 0))
    return pl.pallas_call(
        _win_attn_kernel,
        grid=(t // seq,),
        in_specs=[pl.BlockSpec(memory_space=pltpu.SMEM),
                  seq_rows(ATTN_Q_DIM), seq_rows(kv_tiled),
                  pl.BlockSpec((n_blocks, kv_tiled, BLOCK), lambda b: (b, 0, 0)),
                  _resident(bucket_t.shape), _resident(sink_rows.shape)],
        out_specs=seq_rows(ATTN_Q_DIM),
        out_shape=jax.ShapeDtypeStruct((t, ATTN_Q_DIM), BF16),
        scratch_shapes=[pltpu.VMEM((3,) + scores_shape, F32)]
                       + [pltpu.VMEM(scores_shape, F32)] * 2
                       + [pltpu.VMEM(scores_shape, BF16)] * 2
                       + [pltpu.VMEM((N_KV_HEADS, 1, GROUP * BLOCK), F32)] * 2,
        compiler_params=_params("arbitrary"),
        name="win_attn",
    )(rel_bias, q, k4, v4t, bucket_t, sink_rows)


def _mem_kv_kernel(m_ref, w_ref, km_ref, vm_ref):
    mb = m_ref[...].astype(BF16)
    km_ref[...] = _dot(mb, w_ref[:, :MEM_DIM]).astype(BF16)
    vm_ref[...] = _dot(mb, w_ref[:, MEM_DIM:]).astype(BF16)


def _mem_kv(mem2, w_mem_kv, tm):
    t = mem2.shape[0]
    out = pl.BlockSpec((tm, MEM_DIM), lambda i: (i, 0))
    return pl.pallas_call(
        _mem_kv_kernel,
        grid=(t // tm,),
        in_specs=[pl.BlockSpec((tm, D_MODEL), lambda i: (i, 0)), _resident(w_mem_kv.shape)],
        out_specs=[out, out],
        out_shape=[jax.ShapeDtypeStruct((t, MEM_DIM), BF16)] * 2,
        compiler_params=_params("parallel"),
        name="mem_kv",
    )(mem2, w_mem_kv)


def _merge_kernel(alpha, tiles_per_seq, chunks, x_ref, u_ref, u_prev_ref, u_next_ref, a_ref, qm_ref,
                  km_ref, vm_ref, cw_ref, cb_ref, cg_ref, cbeta_ref, win_ref, bg_ref,
                  wc_ref, wa_ref, wm_ref, wo_ref, g_ref, b_ref, *refs):
    n_cast = len(chunks)
    cast_srcs, h_ref = refs[:n_cast], refs[n_cast]
    cast_dsts, (seq_ref, taps_ref) = refs[n_cast + 1:2 * n_cast + 1], refs[2 * n_cast + 1:]
    _cast_side_job(chunks, cast_srcs, cast_dsts)

    x = x_ref[...]
    xb = x.astype(BF16)
    tile_in_seq = pl.program_id(0) % tiles_per_seq
    c = _tile_conv(u_ref, u_prev_ref, u_next_ref, cw_ref, cb_ref, cg_ref, cbeta_ref,
                   tile_in_seq == 0, tile_in_seq == tiles_per_seq - 1, seq_ref, taps_ref)

    mem_scale = MEM_HEAD_DIM ** -0.5
    heads = []
    for h in range(N_MEM_HEADS):
        lanes = slice(h * MEM_HEAD_DIM, (h + 1) * MEM_HEAD_DIM)
        s = lax.dot_general(qm_ref[:, lanes], km_ref[:, lanes], (((1,), (1,)), ((), ())),
                            preferred_element_type=F32) * mem_scale
        p = jnp.exp(s - jnp.max(s, axis=-1, keepdims=True))
        denom = jnp.sum(p, axis=-1, keepdims=True)
        heads.append(_dot(p.astype(BF16), vm_ref[:, lanes]) * (1.0 / denom))
    mo = jnp.concatenate(heads, axis=1).astype(BF16)

    def gate(i):
        cols = slice(i * D_MODEL, (i + 1) * D_MODEL)
        w_cols = slice(PROJ_DIM + i * D_MODEL, PROJ_DIM + (i + 1) * D_MODEL)
        return jax.nn.sigmoid(_dot(xb, win_ref[:, w_cols]) + bg_ref[:, cols])

    merged = gate(0) * _dot(c.astype(BF16), wc_ref[...])
    merged = merged + gate(1) * _dot(a_ref[...], wa_ref[...])
    merged = merged + gate(2) * _dot(mo, wm_ref[...])
    merged = merged.astype(BF16)
    for r in range(0, x_ref.shape[0], _EPILOGUE_ROWS):
        rows = slice(r, r + _EPILOGUE_ROWS)
        out = _dot(merged[rows, :], wo_ref[...])
        h_ref[rows, :] = _layer_norm(alpha * x_ref[rows, :] + out, g_ref[...], b_ref[...])


def _merge(alpha, x2, u, a, qm, km, vm, conv_w, conv_b, conv_g, conv_beta, w_in, bg, wc, wa, wm, wo,
           ln_g, ln_b, to_cast, seq, tm):
    t = x2.shape[0]
    n_steps = t // tm
    tiles_per_seq = seq // tm
    groups = tm // V7X_SUBLANES
    edge_blocks = tm // CONV_EDGE_ROWS
    assert groups >= CONV_HALO and groups % _CONV_GROUPS == 0 and CONV_EDGE_ROWS >= CONV_HALO
    row = lambda width: pl.BlockSpec((tm, width), lambda i: (i, 0))
    edge = lambda index_map: pl.BlockSpec((CONV_EDGE_ROWS, CONV_DIM), index_map)
    mem_spec = pl.BlockSpec((N_MEM, MEM_DIM), lambda i: (i // tiles_per_seq, 0))
    weights = (conv_w, conv_b, conv_g, conv_beta, w_in, bg, wc, wa, wm, wo, ln_g, ln_b)
    chunks, cast_specs, cast_shapes = _cast_plan(to_cast, n_steps)
    outs = pl.pallas_call(
        functools.partial(_merge_kernel, alpha, tiles_per_seq, chunks),
        grid=(n_steps,),
        in_specs=[row(D_MODEL),
                  pl.BlockSpec((V7X_SUBLANES, groups, CONV_DIM), lambda i: (i, 0, 0)),
                  edge(lambda i: (jnp.maximum(i * edge_blocks - 1, 0), 0)),
                  edge(lambda i: (jnp.minimum((i + 1) * edge_blocks, t // CONV_EDGE_ROWS - 1), 0)),
                  row(ATTN_Q_DIM), row(MEM_DIM), mem_spec, mem_spec]
                 + [_resident(w.shape) for w in weights] + cast_specs,
        out_specs=[row(D_MODEL)] + cast_specs,
        out_shape=[jax.ShapeDtypeStruct((t, D_MODEL), F32)] + cast_shapes,
        scratch_shapes=[pltpu.VMEM((groups + 2 * CONV_HALO, V7X_SUBLANES, CONV_DIM), F32),
                        pltpu.VMEM((CONV_WIDTH, V7X_SUBLANES, CONV_DIM), F32)],
        compiler_params=_params("arbitrary"),
        name="merge",
    )(x2, u.reshape(t // groups, groups, CONV_DIM), u, u, a, qm, km, vm, *weights, *to_cast)
    return outs[0], outs[1:]


def _ffn_kernel(alpha, chunk, h_ref, w1_ref, w2_ref, g_ref, b_ref, o_ref, act_ref):
    h = h_ref[...]
    hb = h.astype(BF16)
    d_ff = w2_ref.shape[0]
    for lo in range(0, d_ff, chunk):
        gate = _dot(hb, w1_ref[:, lo:lo + chunk])
        up = _dot(hb, w1_ref[:, d_ff + lo:d_ff + lo + chunk])
        act_ref[:, lo:lo + chunk] = (gate * jax.nn.sigmoid(gate) * up).astype(BF16)
    for r in range(0, h_ref.shape[0], _EPILOGUE_ROWS):
        rows = slice(r, r + _EPILOGUE_ROWS)
        ffn = _dot(act_ref[rows, :], w2_ref[...])
        o_ref[rows, :] = _layer_norm(alpha * h_ref[rows, :] + ffn, g_ref[...], b_ref[...])


def _ffn(alpha, h, w1, w2, ln_g, ln_b, tm, chunk):
    t = h.shape[0]
    d_ff = w2.shape[0]
    row = pl.BlockSpec((tm, D_MODEL), lambda i: (i, 0))
    weights = (w1, w2, ln_g, ln_b)
    return pl.pallas_call(
        functools.partial(_ffn_kernel, alpha, chunk),
        grid=(t // tm,),
        in_specs=[row] + [_resident(w.shape) for w in weights],
        out_specs=row,
        out_shape=jax.ShapeDtypeStruct((t, D_MODEL), F32),
        scratch_shapes=[pltpu.VMEM((tm, d_ff), BF16)],
        compiler_params=_params("parallel"),
        name="ffn",
    )(h, *weights)


def _t5_bucket(rel):
    half = N_BUCKETS // 2
    max_exact = half // 2
    base = jnp.where(rel > 0, half, 0)
    n = jnp.abs(rel)
    nf = jnp.maximum(n, 1).astype(jnp.float32)
    large = max_exact + (jnp.log(nf / max_exact) / math.log(MAX_DISTANCE / max_exact)
                         * (half - max_exact)).astype(jnp.int32)
    large = jnp.minimum(large, half - 1)
    return base + jnp.where(n < max_exact, n, large)


def _layer(x2, mem2, bucket, rel_bias, seq, alpha, w_in, b_gate, conv_dw_w, conv_dw_b, conv_ln_g,
           conv_ln_b, w_conv_out, attn_sink, w_attn_out, w_mem_kv, w_mem_out, w_o, ln1_g, ln1_b,
           w_ffn_in, w_ffn_out, ln2_g, ln2_b):
    row = lambda v: v.reshape(1, -1)
    (u, q, k4, v4t, qm), (w_in_b, wc, wa, wm, wo, wkv) = _in_proj(
        x2, w_in, (w_in, w_conv_out, w_attn_out, w_mem_out, w_o, w_mem_kv), tm=512)
    sink_rows = jnp.repeat(attn_sink.astype(F32).reshape(N_KV_HEADS, GROUP), BLOCK, axis=1)[:, None, :]
    a = _win_attn(q, k4, v4t, rel_bias.astype(F32), bucket, sink_rows, seq)
    km, vm = _mem_kv(mem2, wkv, tm=512)
    h, (w1, w2) = _merge(alpha, x2, u, a, qm, km, vm, conv_dw_w, row(conv_dw_b), row(conv_ln_g),
                         row(conv_ln_b), w_in_b, row(b_gate), wc, wa, wm, wo, row(ln1_g), row(ln1_b),
                         (w_ffn_in, w_ffn_out), seq, tm=512)
    return _ffn(alpha, h, w1, w2, row(ln2_g), row(ln2_b), tm=1024, chunk=256)


def kernel(x, mem, rel_bias, w_in, b_gate, conv_dw_w, conv_dw_b, conv_ln_g, conv_ln_b, w_conv_out,
           attn_sink, w_attn_out, w_mem_kv, w_mem_out, w_o, ln1_g, ln1_b, w_ffn_in, w_ffn_out,
           ln2_g, ln2_b):
    batch, seq, d_model = x.shape
    depth = w_in.shape[0]
    assert d_model == D_MODEL and mem.shape[1:] == (N_MEM, D_MODEL) and seq % BLOCK == 0
    alpha = (2 * depth) ** 0.25
    qloc = jnp.arange(BLOCK, dtype=jnp.int32)
    kloc = jnp.arange(3 * BLOCK, dtype=jnp.int32) - BLOCK
    bucket = _t5_bucket(kloc[:, None] - qloc[None, :])
    x2 = x.reshape(batch * seq, d_model)
    mem2 = mem.reshape(batch * N_MEM, d_model)
    per_layer = (w_in, b_gate, conv_dw_w, conv_dw_b, conv_ln_g, conv_ln_b, w_conv_out, attn_sink,
                 w_attn_out, w_mem_kv, w_mem_out, w_o, ln1_g, ln1_b, w_ffn_in, w_ffn_out, ln2_g, ln2_b)
    for l in range(depth):
        x2 = _layer(x2, mem2, bucket, rel_bias, seq, alpha, *(p[l] for p in per_layer))
    return x2.reshape(batch, seq, d_model)
```

```python
import functools
import math

import jax
import jax.numpy as jnp
from jax import lax
from jax.experimental import pallas as pl
from jax.experimental.pallas import tpu as pltpu

D_MODEL = 1024
N_MEM = 256
CONV_DIM = 512
CONV_WIDTH = 31
N_HEADS = 8
N_KV_HEADS = 2
HEAD_DIM = 64
WINDOW = 128
BLOCK = 128
N_MEM_HEADS = 4
MEM_HEAD_DIM = 128
N_BUCKETS = 32
MAX_DISTANCE = 128
N_BRANCHES = 3
LN_EPS = 1e-5
NEG_INF = -1e30

GROUP = N_HEADS // N_KV_HEADS
ATTN_Q_DIM = N_HEADS * HEAD_DIM
KV_DIM = N_KV_HEADS * HEAD_DIM
MEM_DIM = N_MEM_HEADS * MEM_HEAD_DIM
GROUP_DIM = GROUP * HEAD_DIM
ATTN_SCALE = HEAD_DIM ** -0.5
LOG2E = math.log2(math.e)
PROJ_DIM = 2 * CONV_DIM + ATTN_Q_DIM + 2 * KV_DIM + MEM_DIM

V7X_LANES = 128
V7X_SUBLANES = 8
V7X_BF16_SUBLANES = 16
V7X_VMEM_LIMIT_BYTES = 56 * 1024 * 1024

BF16 = jnp.bfloat16
F32 = jnp.float32


def _params(*semantics):
    return pltpu.CompilerParams(dimension_semantics=semantics,
                                vmem_limit_bytes=V7X_VMEM_LIMIT_BYTES)


def _resident(shape):
    nd = len(shape)
    return pl.BlockSpec(shape, lambda *_: (0,) * nd, pipeline_mode=pl.Buffered(1))


def _dot(a, b):
    return jnp.dot(a, b, preferred_element_type=F32)


_EPILOGUE_ROWS = 256


def _layer_norm(v, g, b):
    mu = jnp.mean(v, axis=-1, keepdims=True)
    d = v - mu
    var = jnp.mean(d * d, axis=-1, keepdims=True)
    return d * lax.rsqrt(var + LN_EPS) * g + b


def _cast_chunks(rows, n_steps):
    return max(n for n in range(1, n_steps + 1)
               if rows % n == 0 and (rows // n) % V7X_BF16_SUBLANES == 0)


def _cast_plan(weights, n_steps):
    chunks = tuple(_cast_chunks(w.shape[0], n_steps) for w in weights)
    specs = [pl.BlockSpec((w.shape[0] // n, w.shape[1]), lambda i, n=n: (jnp.minimum(i, n - 1), 0))
             for w, n in zip(weights, chunks)]
    shapes = [jax.ShapeDtypeStruct(w.shape, BF16) for w in weights]
    return chunks, specs, shapes


def _cast_side_job(chunks, src_refs, dst_refs):
    step = pl.program_id(0)
    for n, src, dst in zip(chunks, src_refs, dst_refs):
        @pl.when(step < n)
        def _():
            dst[...] = src[...].astype(BF16)


def _in_proj_kernel(chunks, x_ref, w_ref, *refs):
    n_cast = len(chunks)
    cast_srcs, refs = refs[:n_cast], refs[n_cast:]
    u_ref, q_ref, k4_ref, v4t_ref, qm_ref = refs[:5]
    cast_dsts, (wb_ref,) = refs[5:5 + n_cast], refs[5 + n_cast:]

    @pl.when(pl.program_id(0) == 0)
    def _():
        wb_ref[...] = w_ref[...].astype(BF16)

    _cast_side_job(chunks, cast_srcs, cast_dsts)
    xb = x_ref[...].astype(BF16)

    def proj(lo, width):
        return _dot(xb, wb_ref[:, lo:lo + width])

    a = proj(0, CONV_DIM)
    g = proj(CONV_DIM, CONV_DIM)
    u_ref[...] = a * jax.nn.sigmoid(g)
    off = 2 * CONV_DIM
    q_ref[...] = (proj(off, ATTN_Q_DIM) * (ATTN_SCALE * LOG2E)).astype(BF16)
    off += ATTN_Q_DIM
    k = proj(off, KV_DIM)
    v = proj(off + KV_DIM, KV_DIM)
    off += 2 * KV_DIM
    qm_ref[...] = proj(off, MEM_DIM).astype(BF16)

    first_half = lax.broadcasted_iota(jnp.int32, k.shape, 1) < HEAD_DIM
    swapped = pltpu.roll(k, HEAD_DIM, axis=1)
    h0 = jnp.where(first_half, k, swapped).astype(BF16)
    h1 = jnp.where(first_half, swapped, k).astype(BF16)
    k4_ref[...] = jnp.concatenate([h0, h0, h1, h1], axis=1)

    vt = v.T.astype(BF16)
    for c in range(v4t_ref.shape[0]):
        keys = vt[:, c * BLOCK:(c + 1) * BLOCK]
        v4t_ref[c] = jnp.concatenate([keys[:HEAD_DIM]] * GROUP + [keys[HEAD_DIM:]] * GROUP, axis=0)


def _in_proj(x2, w_in, to_cast, tm):
    t = x2.shape[0]
    n_steps = t // tm
    row = lambda width: pl.BlockSpec((tm, width), lambda i: (i, 0))
    kv_tiled = N_KV_HEADS * GROUP_DIM
    chunks, cast_specs, cast_shapes = _cast_plan(to_cast, n_steps)
    out_specs = [row(CONV_DIM), row(ATTN_Q_DIM), row(kv_tiled),
                 pl.BlockSpec((tm // BLOCK, kv_tiled, BLOCK), lambda i: (i, 0, 0)), row(MEM_DIM)]
    out_shape = [jax.ShapeDtypeStruct((t, CONV_DIM), F32),
                 jax.ShapeDtypeStruct((t, ATTN_Q_DIM), BF16),
                 jax.ShapeDtypeStruct((t, kv_tiled), BF16),
                 jax.ShapeDtypeStruct((t // BLOCK, kv_tiled, BLOCK), BF16),
                 jax.ShapeDtypeStruct((t, MEM_DIM), BF16)]
    outs = pl.pallas_call(
        functools.partial(_in_proj_kernel, chunks),
        grid=(n_steps,),
        in_specs=[row(D_MODEL),
                  pl.BlockSpec((D_MODEL, PROJ_DIM), lambda i: (0, 0), pipeline_mode=pl.Buffered(1))]
                 + cast_specs,
        out_specs=out_specs + cast_specs,
        out_shape=out_shape + cast_shapes,
        scratch_shapes=[pltpu.VMEM((D_MODEL, PROJ_DIM), BF16)],
        compiler_params=_params("arbitrary"),
        name="in_proj",
    )(x2, w_in, *to_cast)
    return outs[:5], outs[5:]


CONV_HALO = CONV_WIDTH // 2
CONV_EDGE_ROWS = 16
_CONV_GROUPS = 8
_GATE_CHUNK = 256


def _tile_conv(u_ref, prev_ref, next_ref, w_ref, b_ref, g_ref, beta_ref, is_first, is_last,
               seq_ref, taps_ref, pace):
    sub, n, _ = u_ref.shape
    for k in range(CONV_WIDTH):
        taps_ref[k] = jnp.broadcast_to(w_ref[k:k + 1, :], (sub, CONV_DIM))
    for a0 in range(0, n, sub):
        seq_ref[CONV_HALO + a0:CONV_HALO + a0 + sub] = pltpu.einshape("rac->arc", u_ref[:, a0:a0 + sub, :])

    phase = lax.broadcasted_iota(jnp.int32, (CONV_HALO, sub, CONV_DIM), 1)
    outside_before = jnp.where(is_first, 0.0, prev_ref[CONV_EDGE_ROWS - CONV_HALO:, :])
    before = pltpu.roll(seq_ref[n:n + CONV_HALO], 1, axis=1)
    seq_ref[0:CONV_HALO] = jnp.where(phase == 0, outside_before[:, None, :], before)
    outside_after = jnp.where(is_last, 0.0, next_ref[:CONV_HALO, :])
    after = pltpu.roll(seq_ref[CONV_HALO:2 * CONV_HALO], sub - 1, axis=1)
    seq_ref[CONV_HALO + n:2 * CONV_HALO + n] = jnp.where(phase == sub - 1, outside_after[:, None, :], after)

    done = []
    for blk, a0 in enumerate(range(0, n, _CONV_GROUPS)):
        first_tap = taps_ref[0] + jnp.concatenate([pace[blk]] * (CONV_DIM // V7X_LANES), axis=1)
        acc = seq_ref[a0:a0 + _CONV_GROUPS] * first_tap
        for k in range(1, CONV_WIDTH):
            acc = acc + seq_ref[a0 + k:a0 + k + _CONV_GROUPS] * taps_ref[k]
        y = _layer_norm(acc + b_ref[...], g_ref[...], beta_ref[...])
        done.append(y * jax.nn.sigmoid(y))
    y = pltpu.einshape("arc->rac", jnp.concatenate(done, axis=0))
    return y.reshape(sub * n, CONV_DIM)


def _win_attn_kernel(rb_ref, q_ref, k4_ref, v4t_ref, bucket_ref, sink_ref, o_ref,
                     bias_ref, s0_ref, s1_ref, p0_ref, p1_ref, inv0_ref, inv1_ref):
    s_ref, p_ref, inv_ref = (s0_ref, s1_ref), (p0_ref, p1_ref), (inv0_ref, inv1_ref)
    n_blocks = v4t_ref.shape[0]
    last = n_blocks - 1
    keys = 3 * BLOCK
    first_var, mid_var, last_var = 0, 1, 2

    @pl.when(pl.program_id(0) == 0)
    def _():
        bucket = bucket_ref[...]
        kpos = lax.broadcasted_iota(jnp.int32, (keys, BLOCK), 0) - BLOCK
        qpos = lax.broadcasted_iota(jnp.int32, (keys, BLOCK), 1)
        in_band = jnp.abs(kpos - qpos) <= WINDOW
        for h in range(N_HEADS):
            acc = jnp.zeros((keys, BLOCK), F32)
            for bkt in range(N_BUCKETS):
                acc = jnp.where(bucket == bkt, rb_ref[bkt, h] * LOG2E, acc)
            acc = jnp.where(in_band, acc, NEG_INF)
            j, g = divmod(h, GROUP)
            cols = slice(g * BLOCK, (g + 1) * BLOCK)
            bias_ref[mid_var, j, :, cols] = acc
            bias_ref[first_var, j, :, cols] = jnp.where(kpos >= 0, acc, NEG_INF)
            bias_ref[last_var, j, :, cols] = jnp.where(kpos < BLOCK, acc, NEG_INF)

    head_of_lane = lax.broadcasted_iota(jnp.int32, (BLOCK, GROUP_DIM), 1) // HEAD_DIM

    def window(n):
        return [jnp.maximum(n - 1, 0), n, jnp.minimum(n + 1, last)]

    def rows_of(n):
        return pl.ds(pl.multiple_of(n * BLOCK, BLOCK), BLOCK)

    halves = [(j, h) for j in range(N_KV_HEADS) for h in range(2)]
    half_cols = GROUP * BLOCK // 2

    def scores_chunks(n, slot):
        def chunk(j, h):
            lanes = slice(j * GROUP_DIM, (j + 1) * GROUP_DIM)
            qg = q_ref[rows_of(n), lanes]
            qm = jnp.concatenate([jnp.where(head_of_lane == g, qg, jnp.zeros_like(qg))
                                  for g in (2 * h, 2 * h + 1)], axis=0)
            kwin = jnp.concatenate([k4_ref[rows_of(b), lanes] for b in window(n)], axis=0)
            s_ref[slot][j, :, h * half_cols:(h + 1) * half_cols] = lax.dot_general(
                kwin, qm, (((1,), (1,)), ((), ())), preferred_element_type=F32)
        return [functools.partial(chunk, j, h) for j, h in halves]

    def softmax_chunks(slot, var):
        def chunk(j, h):
            for g in (2 * h, 2 * h + 1):
                cols = slice(g * BLOCK, (g + 1) * BLOCK)
                s = s_ref[slot][j, :, cols] + bias_ref[var, j, :, cols]
                sink = sink_ref[j, :, cols] * LOG2E
                m = jnp.maximum(jnp.max(s, axis=0, keepdims=True), sink)
                p = jnp.exp2(s - m)
                denom = jnp.sum(p, axis=0, keepdims=True) + jnp.exp2(sink - m)
                p_ref[slot][j, :, cols] = p.astype(BF16)
                inv_ref[slot][j, :, cols] = 1.0 / denom
        return [functools.partial(chunk, j, h) for j, h in halves]

    def values_chunks(n, slot):
        heads = []

        def chunk(j, h):
            lanes = slice(j * GROUP_DIM, (j + 1) * GROUP_DIM)
            vwin = jnp.concatenate([v4t_ref[b, lanes, :] for b in window(n)], axis=1)
            pv = _dot(vwin, p_ref[slot][j, :, h * half_cols:(h + 1) * half_cols])
            for i, g in enumerate((2 * h, 2 * h + 1)):
                inv = inv_ref[slot][j, :, g * BLOCK:(g + 1) * BLOCK]
                heads.append(pv[g * HEAD_DIM:(g + 1) * HEAD_DIM, i * BLOCK:(i + 1) * BLOCK] * inv)
            if len(heads) == N_HEADS:
                o_ref[rows_of(n), :] = jnp.concatenate(heads, axis=0).T.astype(BF16)
        return [functools.partial(chunk, j, h) for j, h in halves]

    def issue(*stages):
        for chunks in zip(*stages):
            for chunk in chunks:
                chunk()

    def two_steps(i, carry):
        for slot in range(2):
            t = 2 * i + slot
            issue(values_chunks(t - 2, slot), softmax_chunks(1 - slot, mid_var), scores_chunks(t, slot))
        return carry

    issue(scores_chunks(0, 0))
    issue(softmax_chunks(0, first_var), scores_chunks(1, 1))
    lax.fori_loop(1, n_blocks // 2, two_steps, 0)
    issue(values_chunks(last - 1, 0), softmax_chunks(1, last_var))
    issue(values_chunks(last, 1))


def _win_attn(q, k4, v4t, rel_bias, bucket_t, sink_rows, seq):
    t = q.shape[0]
    n_blocks = seq // BLOCK
    assert n_blocks >= 2 and n_blocks % 2 == 0
    scores_shape = (N_KV_HEADS, 3 * BLOCK, GROUP * BLOCK)
    kv_tiled = N_KV_HEADS * GROUP_DIM
    seq_rows = lambda width: pl.BlockSpec((seq, width), lambda b: (b, 0))
    return pl.pallas_call(
        _win_attn_kernel,
        grid=(t // seq,),
        in_specs=[pl.BlockSpec(memory_space=pltpu.SMEM),
                  seq_rows(ATTN_Q_DIM), seq_rows(kv_tiled),
                  pl.BlockSpec((n_blocks, kv_tiled, BLOCK), lambda b: (b, 0, 0)),
                  _resident(bucket_t.shape), _resident(sink_rows.shape)],
        out_specs=seq_rows(ATTN_Q_DIM),
        out_shape=jax.ShapeDtypeStruct((t, ATTN_Q_DIM), BF16),
        scratch_shapes=[pltpu.VMEM((3,) + scores_shape, F32)]
                       + [pltpu.VMEM(scores_shape, F32)] * 2
                       + [pltpu.VMEM(scores_shape, BF16)] * 2
                       + [pltpu.VMEM((N_KV_HEADS, 1, GROUP * BLOCK), F32)] * 2,
        compiler_params=_params("arbitrary"),
        name="win_attn",
    )(rel_bias, q, k4, v4t, bucket_t, sink_rows)


def _mem_kv_kernel(m_ref, w_ref, km_ref, vm_ref):
    mb = m_ref[...].astype(BF16)
    km_ref[...] = _dot(mb, w_ref[:, :MEM_DIM]).astype(BF16)
    vm_ref[...] = _dot(mb, w_ref[:, MEM_DIM:]).astype(BF16)


def _mem_kv(mem2, w_mem_kv, tm):
    t = mem2.shape[0]
    out = pl.BlockSpec((tm, MEM_DIM), lambda i: (i, 0))
    return pl.pallas_call(
        _mem_kv_kernel,
        grid=(t // tm,),
        in_specs=[pl.BlockSpec((tm, D_MODEL), lambda i: (i, 0)), _resident(w_mem_kv.shape)],
        out_specs=[out, out],
        out_shape=[jax.ShapeDtypeStruct((t, MEM_DIM), BF16)] * 2,
        compiler_params=_params("parallel"),
        name="mem_kv",
    )(mem2, w_mem_kv)


def _merge_kernel(alpha, tiles_per_seq, chunks, x_ref, u_ref, u_prev_ref, u_next_ref, a_ref, qm_ref,
                  km_ref, vm_ref, cw_ref, cb_ref, cg_ref, cbeta_ref, win_ref, bg_ref,
                  wc_ref, wa_ref, wm_ref, wo_ref, g_ref, b_ref, *refs):
    n_cast = len(chunks)
    cast_srcs, h_ref = refs[:n_cast], refs[n_cast]
    cast_dsts, (seq_ref, taps_ref) = refs[n_cast + 1:2 * n_cast + 1], refs[2 * n_cast + 1:]
    _cast_side_job(chunks, cast_srcs, cast_dsts)

    x = x_ref[...]
    xb = x.astype(BF16)
    tile_in_seq = pl.program_id(0) % tiles_per_seq
    logits = [_dot(xb, win_ref[:, PROJ_DIM + lo:PROJ_DIM + lo + _GATE_CHUNK])
              for lo in range(0, N_BRANCHES * D_MODEL, _GATE_CHUNK)]
    never = pl.program_id(0) < 0
    pace = [jnp.where(never, l[:V7X_SUBLANES, :V7X_LANES], 0.0) for l in logits]
    c = _tile_conv(u_ref, u_prev_ref, u_next_ref, cw_ref, cb_ref, cg_ref, cbeta_ref,
                   tile_in_seq == 0, tile_in_seq == tiles_per_seq - 1, seq_ref, taps_ref, pace)
    gates = jax.nn.sigmoid(jnp.concatenate(logits, axis=1) + bg_ref[...])

    mem_scale = MEM_HEAD_DIM ** -0.5
    heads = []
    for h in range(N_MEM_HEADS):
        lanes = slice(h * MEM_HEAD_DIM, (h + 1) * MEM_HEAD_DIM)
        s = lax.dot_general(qm_ref[:, lanes], km_ref[:, lanes], (((1,), (1,)), ((), ())),
                            preferred_element_type=F32) * mem_scale
        p = jnp.exp(s - jnp.max(s, axis=-1, keepdims=True))
        denom = jnp.sum(p, axis=-1, keepdims=True)
        heads.append(_dot(p.astype(BF16), vm_ref[:, lanes]) * (1.0 / denom))
    mo = jnp.concatenate(heads, axis=1).astype(BF16)

    def gate(i):
        return gates[:, i * D_MODEL:(i + 1) * D_MODEL]

    merged = gate(0) * _dot(c.astype(BF16), wc_ref[...])
    merged = merged + gate(1) * _dot(a_ref[...], wa_ref[...])
    merged = merged + gate(2) * _dot(mo, wm_ref[...])
    merged = merged.astype(BF16)
    for r in range(0, x_ref.shape[0], _EPILOGUE_ROWS):
        rows = slice(r, r + _EPILOGUE_ROWS)
        out = _dot(merged[rows, :], wo_ref[...])
        h_ref[rows, :] = _layer_norm(alpha * x_ref[rows, :] + out, g_ref[...], b_ref[...])


def _merge(alpha, x2, u, a, qm, km, vm, conv_w, conv_b, conv_g, conv_beta, w_in, bg, wc, wa, wm, wo,
           ln_g, ln_b, to_cast, seq, tm):
    t = x2.shape[0]
    n_steps = t // tm
    tiles_per_seq = seq // tm
    groups = tm // V7X_SUBLANES
    edge_blocks = tm // CONV_EDGE_ROWS
    assert groups >= CONV_HALO and groups % _CONV_GROUPS == 0 and CONV_EDGE_ROWS >= CONV_HALO
    assert groups // _CONV_GROUPS <= N_BRANCHES * D_MODEL // _GATE_CHUNK
    row = lambda width: pl.BlockSpec((tm, width), lambda i: (i, 0))
    edge = lambda index_map: pl.BlockSpec((CONV_EDGE_ROWS, CONV_DIM), index_map)
    mem_spec = pl.BlockSpec((N_MEM, MEM_DIM), lambda i: (i // tiles_per_seq, 0))
    weights = (conv_w, conv_b, conv_g, conv_beta, w_in, bg, wc, wa, wm, wo, ln_g, ln_b)
    chunks, cast_specs, cast_shapes = _cast_plan(to_cast, n_steps)
    outs = pl.pallas_call(
        functools.partial(_merge_kernel, alpha, tiles_per_seq, chunks),
        grid=(n_steps,),
        in_specs=[row(D_MODEL),
                  pl.BlockSpec((V7X_SUBLANES, groups, CONV_DIM), lambda i: (i, 0, 0)),
                  edge(lambda i: (jnp.maximum(i * edge_blocks - 1, 0), 0)),
                  edge(lambda i: (jnp.minimum((i + 1) * edge_blocks, t // CONV_EDGE_ROWS - 1), 0)),
                  row(ATTN_Q_DIM), row(MEM_DIM), mem_spec, mem_spec]
                 + [_resident(w.shape) for w in weights] + cast_specs,
        out_specs=[row(D_MODEL)] + cast_specs,
        out_shape=[jax.ShapeDtypeStruct((t, D_MODEL), F32)] + cast_shapes,
        scratch_shapes=[pltpu.VMEM((groups + 2 * CONV_HALO, V7X_SUBLANES, CONV_DIM), F32),
                        pltpu.VMEM((CONV_WIDTH, V7X_SUBLANES, CONV_DIM), F32)],
        compiler_params=_params("arbitrary"),
        name="merge",
    )(x2, u.reshape(t // groups, groups, CONV_DIM), u, u, a, qm, km, vm, *weights, *to_cast)
    return outs[0], outs[1:]


def _ffn_kernel(alpha, chunk, h_ref, w1_ref, w2_ref, g_ref, b_ref, o_ref, act_ref):
    h = h_ref[...]
    hb = h.astype(BF16)
    d_ff = w2_ref.shape[0]
    for lo in range(0, d_ff, chunk):
        gate = _dot(hb, w1_ref[:, lo:lo + chunk])
        up = _dot(hb, w1_ref[:, d_ff + lo:d_ff + lo + chunk])
        act_ref[:, lo:lo + chunk] = (gate * jax.nn.sigmoid(gate) * up).astype(BF16)
    for r in range(0, h_ref.shape[0], _EPILOGUE_ROWS):
        rows = slice(r, r + _EPILOGUE_ROWS)
        ffn = _dot(act_ref[rows, :], w2_ref[...])
        o_ref[rows, :] = _layer_norm(alpha * h_ref[rows, :] + ffn, g_ref[...], b_ref[...])


def _ffn(alpha, h, w1, w2, ln_g, ln_b, tm, chunk):
    t = h.shape[0]
    d_ff = w2.shape[0]
    row = pl.BlockSpec((tm, D_MODEL), lambda i: (i, 0))
    weights = (w1, w2, ln_g, ln_b)
    return pl.pallas_call(
        functools.partial(_ffn_kernel, alpha, chunk),
        grid=(t // tm,),
        in_specs=[row] + [_resident(w.shape) for w in weights],
        out_specs=row,
        out_shape=jax.ShapeDtypeStruct((t, D_MODEL), F32),
        scratch_shapes=[pltpu.VMEM((tm, d_ff), BF16)],
        compiler_params=_params("parallel"),
        name="ffn",
    )(h, *weights)


def _t5_bucket(rel):
    half = N_BUCKETS // 2
    max_exact = half // 2
    base = jnp.where(rel > 0, half, 0)
    n = jnp.abs(rel)
    nf = jnp.maximum(n, 1).astype(jnp.float32)
    large = max_exact + (jnp.log(nf / max_exact) / math.log(MAX_DISTANCE / max_exact)
                         * (half - max_exact)).astype(jnp.int32)
    large = jnp.minimum(large, half - 1)
    return base + jnp.where(n < max_exact, n, large)


def _layer(x2, mem2, bucket, rel_bias, seq, alpha, w_in, b_gate, conv_dw_w, conv_dw_b, conv_ln_g,
           conv_ln_b, w_conv_out, attn_sink, w_attn_out, w_mem_kv, w_mem_out, w_o, ln1_g, ln1_b,
           w_ffn_in, w_ffn_out, ln2_g, ln2_b):
    row = lambda v: v.reshape(1, -1)
    (u, q, k4, v4t, qm), (w_in_b, wc, wa, wm, wo, wkv) = _in_proj(
        x2, w_in, (w_in, w_conv_out, w_attn_out, w_mem_out, w_o, w_mem_kv), tm=512)
    sink_rows = jnp.repeat(attn_sink.astype(F32).reshape(N_KV_HEADS, GROUP), BLOCK, axis=1)[:, None, :]
    a = _win_attn(q, k4, v4t, rel_bias.astype(F32), bucket, sink_rows, seq)
    km, vm = _mem_kv(mem2, wkv, tm=512)
    h, (w1, w2) = _merge(alpha, x2, u, a, qm, km, vm, conv_dw_w, row(conv_dw_b), row(conv_ln_g),
                         row(conv_ln_b), w_in_b, row(b_gate), wc, wa, wm, wo, row(ln1_g), row(ln1_b),
                         (w_ffn_in, w_ffn_out), seq, tm=512)
    return _ffn(alpha, h, w1, w2, row(ln2_g), row(ln2_b), tm=1024, chunk=256)


def kernel(x, mem, rel_bias, w_in, b_gate, conv_dw_w, conv_dw_b, conv_ln_g, conv_ln_b, w_conv_out,
           attn_sink, w_attn_out, w_mem_kv, w_mem_out, w_o, ln1_g, ln1_b, w_ffn_in, w_ffn_out,
           ln2_g, ln2_b):
    batch, seq, d_model = x.shape
    depth = w_in.shape[0]
    assert d_model == D_MODEL and mem.shape[1:] == (N_MEM, D_MODEL) and seq % BLOCK == 0
    alpha = (2 * depth) ** 0.25
    qloc = jnp.arange(BLOCK, dtype=jnp.int32)
    kloc = jnp.arange(3 * BLOCK, dtype=jnp.int32) - BLOCK
    bucket = _t5_bucket(kloc[:, None] - qloc[None, :])
    x2 = x.reshape(batch * seq, d_model)
    mem2 = mem.reshape(batch * N_MEM, d_model)
    per_layer = (w_in, b_gate, conv_dw_w, conv_dw_b, conv_ln_g, conv_ln_b, w_conv_out, attn_sink,
                 w_attn_out, w_mem_kv, w_mem_out, w_o, ln1_g, ln1_b, w_ffn_in, w_ffn_out, ln2_g, ln2_b)
    for l in range(depth):
        x2 = _layer(x2, mem2, bucket, rel_bias, seq, alpha, *(p[l] for p in per_layer))
    return x2.reshape(batch, seq, d_model)
```
